```python
import math
import jax, jax.numpy as jnp
from jax import lax
import numpy as np

D_MODEL = 4096
BATCH = 4
SEQ = 4096
DEPTH = 4

N_MIXERS = 2
MEM_WIDTH = D_MODEL // 4
MEM_HEADS = 4
MEM_HEAD_DIM = MEM_WIDTH // MEM_HEADS
N_MEM = 256
MIX_WIDTH = D_MODEL - MEM_WIDTH
HEAD_DIM = 128
N_HEADS = MIX_WIDTH // HEAD_DIM
CHUNK = 64
Q_BLOCK = 128
FFN_DIM = 256 * ((8 * D_MODEL // 3 + 255) // 256)
CONV_W = 3
EPS = 1e-6
F_FLOOR = 1e-20
N_A = (DEPTH + 1) // 2
N_B = DEPTH // 2

kernel_name = "hgrn2_stickbreaking_memxattn_convffn_hybrid"


def rmsnorm(x, g):
    xf = x.astype(jnp.float32)
    y = xf * lax.rsqrt(jnp.mean(xf * xf, axis=-1, keepdims=True) + EPS)
    return (y * g.astype(jnp.float32)).astype(x.dtype)


def heads(t, n, d):
    return t.reshape(t.shape[0], t.shape[1], n, d)


def hgrn2_chunkwise(q, k, v, log_f):
    B, S, H, Dk = q.shape
    Dv = v.shape[-1]
    n = S // CHUNK

    def to_chunks(t):
        return t.astype(jnp.float32).reshape(B, n, CHUNK, H, t.shape[-1]).transpose(1, 0, 3, 2, 4)

    causal = jnp.tril(jnp.ones((CHUNK, CHUNK), dtype=bool))[:, :, None]

    def step(state, inp):
        qc, kc, vc, gc = inp
        b = jnp.cumsum(gc, axis=2)
        inter = jnp.einsum('bhtk,bhkv->bhtv', qc * jnp.exp(b), state)
        diff = b[:, :, :, None, :] - b[:, :, None, :, :]
        decay = jnp.where(causal, jnp.exp(jnp.where(causal, diff, 0.0)), 0.0)
        scores = jnp.einsum('bhtk,bhsk,bhtsk->bhts', qc, kc, decay)
        intra = jnp.einsum('bhts,bhsv->bhtv', scores, vc)
        b_end = b[:, :, -1:, :]
        new_state = (jnp.exp(b_end[:, :, 0, :, None]) * state
                     + jnp.einsum('bhsk,bhsv->bhkv', kc * jnp.exp(b_end - b), vc))
        return new_state, inter + intra

    s0 = jnp.zeros((B, H, Dk, Dv), jnp.float32)
    _, out = lax.scan(step, s0, (to_chunks(q), to_chunks(k), to_chunks(v), to_chunks(log_f)))
    return out.transpose(1, 0, 3, 2, 4).reshape(B, S, H, Dv)


def stick_breaking_attention(q, k, v):
    B, S, H, D = q.shape
    nb = S // Q_BLOCK
    scale = 1.0 / math.sqrt(D)
    qb = q.reshape(B, nb, Q_BLOCK, H, D).transpose(1, 0, 3, 2, 4)
    kh = k.transpose(0, 2, 1, 3)
    vh = v.transpose(0, 2, 1, 3)
    key_pos = jnp.arange(S)

    def block(args):
        qblk, start = args
        z = jnp.einsum('bhqd,bhsd->bhqs', qblk, kh).astype(jnp.float32) * scale
        mask = key_pos[None, :] < (start + jnp.arange(Q_BLOCK))[:, None]
        log_rem = jnp.where(mask, jax.nn.log_sigmoid(-z), 0.0)
        later = lax.cumsum(log_rem, axis=3, reverse=True) - log_rem
        w = jnp.where(mask, jnp.exp(jax.nn.log_sigmoid(z) + later), 0.0)
        return jnp.einsum('bhqs,bhsd->bhqd', w.astype(vh.dtype), vh)

    out = lax.map(block, (qb, jnp.arange(nb) * Q_BLOCK))
    return out.transpose(1, 0, 3, 2, 4).reshape(B, S, H, D)


def memory_attention(qm, mk, mv):
    s = jnp.einsum('bshd,bmhd->bhsm', qm, mk).astype(jnp.float32) * (MEM_HEAD_DIM ** -0.5)
    p = jax.nn.softmax(s, axis=-1).astype(mv.dtype)
    return jnp.einsum('bhsm,bmhd->bshd', p, mv)


def setup_inputs(seed: int = 0) -> dict:
    key = jax.random.key(seed)
    ks = jax.random.split(key, 20)

    def w(k, shape, fan_in):
        return jax.random.normal(k, shape, jnp.float32) * (fan_in ** -0.5)

    def gain(k, shape):
        return 1.0 + 0.05 * jax.random.normal(k, shape, jnp.float32)

    return {
        "x": jax.random.normal(ks[0], (BATCH, SEQ, D_MODEL), jnp.float32),
        "mem": jax.random.normal(ks[1], (BATCH, N_MEM, D_MODEL), jnp.float32),
        "mem_norm": gain(ks[2], (D_MODEL,)),
        "mem_w_kv": w(ks[3], (D_MODEL, 2 * MEM_WIDTH), D_MODEL),
        "lb_logits": 0.5 * jax.random.normal(ks[4], (DEPTH, MIX_WIDTH), jnp.float32),
        "mix_norm": gain(ks[5], (DEPTH, D_MODEL)),
        "hgrn_w_in": w(ks[6], (N_A, D_MODEL, 4 * MIX_WIDTH + MEM_WIDTH), D_MODEL),
        "hgrn_norm": gain(ks[7], (N_A, MIX_WIDTH)),
        "hgrn_w_out": w(ks[8], (N_A, D_MODEL, D_MODEL), D_MODEL),
        "sb_w_in": w(ks[9], (N_B, D_MODEL, 3 * MIX_WIDTH + MEM_WIDTH), D_MODEL),
        "sb_w_out": w(ks[10], (N_B, D_MODEL, D_MODEL), D_MODEL),
        "ffn_norm": gain(ks[11], (DEPTH, D_MODEL)),
        "ffn_w_up": w(ks[12], (DEPTH, D_MODEL, 2 * FFN_DIM), D_MODEL),
        "ffn_conv_w": w(ks[13], (DEPTH, CONV_W, 2 * FFN_DIM), CONV_W),
        "ffn_conv_b": 0.01 * jax.random.normal(ks[14], (DEPTH, 2 * FFN_DIM), jnp.float32),
        "ffn_w_down": w(ks[15], (DEPTH, FFN_DIM, D_MODEL), FFN_DIM),
        "final_norm": gain(ks[16], (D_MODEL,)),
    }


def reference(x, mem, mem_norm, mem_w_kv, lb_logits, mix_norm, hgrn_w_in, hgrn_norm, hgrn_w_out,
              sb_w_in, sb_w_out, ffn_norm, ffn_w_up, ffn_conv_w, ffn_conv_b, ffn_w_down, final_norm):
    B, S, _ = x.shape
    f32 = jnp.float32

    mk, mv = jnp.split(rmsnorm(mem, mem_norm) @ mem_w_kv, 2, axis=-1)
    mk = heads(mk, MEM_HEADS, MEM_HEAD_DIM)
    mv = heads(mv, MEM_HEADS, MEM_HEAD_DIM)

    lb_p = jax.nn.softmax(lb_logits.astype(f32), axis=0)
    lower = jnp.maximum(jnp.cumsum(lb_p, axis=0) - lb_p[0], 0.0)

    for i in range(DEPTH):
        j = i // N_MIXERS
        h = rmsnorm(x, mix_norm[i])
        if i % N_MIXERS == 0:
            q, zf, vi, zg, qm = jnp.split(
                h @ hgrn_w_in[j], [MIX_WIDTH, 2 * MIX_WIDTH, 3 * MIX_WIDTH, 4 * MIX_WIDTH], axis=-1)
            zf = zf.astype(f32)
            lb = lower[i]
            f = lb + (1.0 - lb) * jax.nn.sigmoid(zf)
            log_f = jnp.log(jnp.maximum(f, F_FLOOR))
            k_in = (1.0 - lb) * jax.nn.sigmoid(-zf)
            o = hgrn2_chunkwise(heads(q, N_HEADS, HEAD_DIM), heads(k_in, N_HEADS, HEAD_DIM),
                                heads(vi, N_HEADS, HEAD_DIM), heads(log_f, N_HEADS, HEAD_DIM))
            o = rmsnorm(o * jax.nn.sigmoid(heads(zg, N_HEADS, HEAD_DIM).astype(f32)),
                        hgrn_norm[j].reshape(N_HEADS, HEAD_DIM))
            w_out = hgrn_w_out[j]
        else:
            q, k, v, qm = jnp.split(h @ sb_w_in[j], [MIX_WIDTH, 2 * MIX_WIDTH, 3 * MIX_WIDTH], axis=-1)
            o = stick_breaking_attention(heads(q, N_HEADS, HEAD_DIM), heads(k, N_HEADS, HEAD_DIM),
                                         heads(v, N_HEADS, HEAD_DIM))
            w_out = sb_w_out[j]
        om = memory_attention(heads(qm, MEM_HEADS, MEM_HEAD_DIM), mk, mv)
        mixed = jnp.concatenate([o.reshape(B, S, MIX_WIDTH).astype(x.dtype),
                                 om.reshape(B, S, MEM_WIDTH).astype(x.dtype)], axis=-1)
        x = x + mixed @ w_out

        u = rmsnorm(x, ffn_norm[i]) @ ffn_w_up[i]
        up = jnp.pad(u, ((0, 0), (CONV_W - 1, 0), (0, 0)))
        c = ffn_conv_b[i] + sum(ffn_conv_w[i, t] * up[:, t:t + S] for t in range(CONV_W))
        gate, val = jnp.split(c, 2, axis=-1)
        x = x + (jax.nn.silu(gate) * val) @ ffn_w_down[i]

    return rmsnorm(x, final_norm)
```

```python
import functools
import math

import numpy as np
import jax
import jax.numpy as jnp
from jax import lax
from jax.experimental import pallas as pl
from jax.experimental.pallas import tpu as pltpu

F32 = jnp.float32
BF16 = jnp.bfloat16

EPS = 1e-6
F_FLOOR = 1e-20
HEAD_DIM = 128
MEM_HEADS = 4
CONV_W = 3
LANES = 128
SUBLANES = 8
VMEM_LIMIT_BYTES = 56 * 1024 * 1024

HGRN_CHUNK = 128
SB_TILE = 128

TILE_PREFS = dict(norm_m=256, mm_m=1024, mm_n=512, up_m=1024, up_n=256, down_m=512, down_n=512,
                  hgrn_l=512, mem_s=512)


def _tile(dim, pref, quantum):
    t = min(pref, dim)
    t -= t % quantum
    while t > quantum and dim % t:
        t -= quantum
    assert t >= quantum and dim % t == 0, (dim, pref, quantum)
    return t


def _params(*sem):
    return pltpu.CompilerParams(dimension_semantics=sem, vmem_limit_bytes=VMEM_LIMIT_BYTES)


def _dot(a, b):
    return jnp.dot(a, b, preferred_element_type=F32)


def _dot_nt(a, b):
    return lax.dot_general(a, b, (((1,), (1,)), ((), ())), preferred_element_type=F32)


def _dot_tn(a, b):
    return lax.dot_general(a, b, (((0,), (0,)), ((), ())), preferred_element_type=F32)


def _rmsnorm_kernel(x_ref, g_ref, o_ref):
    x = x_ref[...]
    ms = jnp.mean(x * x, axis=-1, keepdims=True)
    o_ref[...] = (x * lax.rsqrt(ms + EPS) * g_ref[...]).astype(o_ref.dtype)


def _rmsnorm(x2d, g, out_dtype):
    T, D = x2d.shape
    tm = _tile(T, TILE_PREFS["norm_m"], SUBLANES)
    return pl.pallas_call(
        _rmsnorm_kernel,
        grid=(T // tm,),
        in_specs=[pl.BlockSpec((tm, D), lambda i: (i, 0)), pl.BlockSpec((1, D), lambda i: (0, 0))],
        out_specs=pl.BlockSpec((tm, D), lambda i: (i, 0)),
        out_shape=jax.ShapeDtypeStruct((T, D), out_dtype),
        compiler_params=_params("parallel"),
        name="rmsnorm",
    )(x2d, g.reshape(1, D).astype(F32))


def _mm_kernel(a_ref, w_ref, o_ref):
    o_ref[...] = _dot(a_ref[...], w_ref[...]).astype(o_ref.dtype)


def _mm_res_kernel(a_ref, w_ref, r_ref, o_ref):
    o_ref[...] = r_ref[...] + _dot(a_ref[...], w_ref[...])


def _matmul(a, w, out_dtype, residual=None, tm_pref=None, tn_pref=None, name="matmul"):
    M, K = a.shape
    N = w.shape[1]
    tm = _tile(M, tm_pref or TILE_PREFS["mm_m"], SUBLANES)
    tn = _tile(N, tn_pref or TILE_PREFS["mm_n"], LANES)
    in_specs = [pl.BlockSpec((tm, K), lambda m, n: (m, 0)), pl.BlockSpec((K, tn), lambda m, n: (0, n))]
    args = [a, w]
    kern = _mm_kernel
    if residual is not None:
        in_specs.append(pl.BlockSpec((tm, tn), lambda m, n: (m, n)))
        args.append(residual)
        kern = _mm_res_kernel
    return pl.pallas_call(
        kern,
        grid=(M // tm, N // tn),
        in_specs=in_specs,
        out_specs=pl.BlockSpec((tm, tn), lambda m, n: (m, n)),
        out_shape=jax.ShapeDtypeStruct((M, N), out_dtype),
        compiler_params=_params("parallel", "arbitrary"),
        name=name,
    )(*args)


def _hgrn_consts(C):
    nlev = int(math.log2(C))
    assert 1 << nlev == C
    r = np.arange(C)
    rr, jj = r[:, None], r[None, :]
    blocks = []
    for l in range(nlev):
        c = 1 << l
        ref = ((r & ~(2 * c - 1)) + c - 1)[:, None]
        odd = ((r & c) != 0)[:, None]
        blocks.append(np.where(odd, (jj > ref) & (jj <= rr), (jj > rr) & (jj <= ref)))
    blocks.append(jj <= rr)
    blocks.append(jj > rr)
    wall = np.concatenate(blocks, axis=0).astype(np.float32)
    x = rr ^ jj
    lvl = np.where(x == 0, -1, np.floor(np.log2(np.maximum(x, 1))).astype(np.int64))
    lvl = np.where(jj > rr, -2, lvl).astype(np.int32)
    return wall, lvl, nlev


def _hgrn_kernel(q_ref, zf_ref, v_ref, zg_ref, lbl_ref, gn_ref, wall_ref, lvl_ref, o_ref, st_ref,
                 *, layer, C, nlev, n_chunks):
    @pl.when(pl.program_id(2) == 0)
    def _():
        st_ref[...] = jnp.zeros_like(st_ref)

    lbl = lbl_ref[...]
    e = jnp.exp(lbl - jnp.max(lbl, axis=0, keepdims=True))
    p = e / jnp.sum(e, axis=0, keepdims=True)
    cs = p[0:1]
    for j in range(1, layer + 1):
        cs = cs + p[j:j + 1]
    lb = jnp.maximum(cs - p[0:1], 0.0)
    gn = gn_ref[...]
    lvl = lvl_ref[...]

    def chunk(c, carry):
        rows = pl.ds(pl.multiple_of(c * C, C), C)
        zf = zf_ref[rows, :]
        q = q_ref[rows, :]
        v = v_ref[rows, :].astype(BF16)
        f = lb + (1.0 - lb) * jax.nn.sigmoid(zf)
        g = jnp.log(jnp.maximum(f, F_FLOOR))
        kin = (1.0 - lb) * jax.nn.sigmoid(-zf)
        g_hi = g.astype(BF16)
        g_lo = (g - g_hi.astype(F32)).astype(BF16)
        dd = _dot(wall_ref[...], jnp.concatenate([g_hi, g_lo], axis=1))
        d = dd[:, :HEAD_DIM] + dd[:, HEAD_DIM:]

        scores = jnp.where(lvl == -1, _dot_nt(q.astype(BF16), kin.astype(BF16)), 0.0)
        for l in range(nlev):
            el = jnp.exp(d[l * C:(l + 1) * C])
            pl_ = _dot_nt((q * el).astype(BF16), (kin * el).astype(BF16))
            scores = jnp.where(lvl == l, pl_, scores)
        b = d[nlev * C:(nlev + 1) * C]
        b_rest = d[(nlev + 1) * C:(nlev + 2) * C]

        st = st_ref[...]
        inter = _dot_nt((q * jnp.exp(b)).astype(BF16), st.astype(BF16))
        intra = _dot(scores.astype(BF16), v)
        o = inter + intra
        st_ref[...] = st * jnp.exp(b[C - 1:C, :]) + _dot_tn(v, (kin * jnp.exp(b_rest)).astype(BF16))

        og = o * jax.nn.sigmoid(zg_ref[rows, :])
        ms = jnp.mean(og * og, axis=-1, keepdims=True)
        o_ref[rows, :] = (og * lax.rsqrt(ms + EPS) * gn).astype(o_ref.dtype)
        return carry

    lax.fori_loop(0, n_chunks, chunk, 0)


def _hgrn_mixer(proj, lb_logits, gnorm, layer, D):
    B, S, _ = proj.shape
    depth, MIX = lb_logits.shape
    H = MIX // HEAD_DIM
    C = min(HGRN_CHUNK, S)
    L = _tile(S, max(TILE_PREFS["hgrn_l"], C), C)
    wall, lvl, nlev = _hgrn_consts(C)
    blk = lambda off: pl.BlockSpec((None, L, HEAD_DIM), lambda b, h, c: (b, c, off + h))
    return pl.pallas_call(
        functools.partial(_hgrn_kernel, layer=layer, C=C, nlev=nlev, n_chunks=L // C),
        grid=(B, H, S // L),
        in_specs=[blk(0), blk(H), blk(2 * H), blk(3 * H),
                  pl.BlockSpec((depth, HEAD_DIM), lambda b, h, c: (0, h)),
                  pl.BlockSpec((1, HEAD_DIM), lambda b, h, c: (0, h)),
                  pl.BlockSpec(wall.shape, lambda b, h, c: (0, 0)),
                  pl.BlockSpec(lvl.shape, lambda b, h, c: (0, 0))],
        out_specs=pl.BlockSpec((None, L, HEAD_DIM), lambda b, h, c: (b, c, h)),
        out_shape=jax.ShapeDtypeStruct((B, S, D), BF16),
        scratch_shapes=[pltpu.VMEM((HEAD_DIM, HEAD_DIM), F32)],
        compiler_params=_params("parallel", "parallel", "arbitrary"),
        name="hgrn2_mixer",
    )(proj, proj, proj, proj, lb_logits.astype(F32), gnorm.reshape(1, MIX).astype(F32),
      jnp.asarray(wall, BF16), jnp.asarray(lvl))


def _sb_consts(T):
    r = np.arange(T)
    u = (r[:, None] > r[None, :]).astype(np.float32)
    return np.concatenate([u, np.ones((T, LANES), np.float32)], axis=1)


def _sb_kernel(q_ref, k_ref, v_ref, u_ref, o_ref, *, T, scale):
    i = pl.program_id(2)
    q = q_ref[...]
    u = u_ref[...]
    strictly_earlier = (lax.broadcasted_iota(jnp.int32, (T, T), 0) > lax.broadcasted_iota(jnp.int32, (T, T), 1))

    def tile(j, carry, acc, diagonal):
        rows = pl.ds(pl.multiple_of(j * T, T), T)
        k = k_ref[rows, :]
        v = v_ref[rows, :]
        z = _dot_nt(q, k) * scale
        sp = jnp.maximum(z, 0.0) + jnp.log(1.0 + jnp.exp(-jnp.abs(z)))
        lr = -sp
        if diagonal:
            lr = jnp.where(strictly_earlier, lr, 0.0)
        hi = lr.astype(BF16)
        lo = (lr - hi.astype(F32)).astype(BF16)
        cs = _dot(jnp.concatenate([hi, lo], axis=0), u)
        cs = cs[:T] + cs[T:]
        w = jnp.exp(z - sp + cs[:, :T] + carry)
        if diagonal:
            w = jnp.where(strictly_earlier, w, 0.0)
        return carry + cs[:, T:], acc + _dot(w.astype(BF16), v)

    carry, acc = tile(i, jnp.zeros((T, LANES), F32), jnp.zeros((T, HEAD_DIM), F32), True)

    def body(jj, ca):
        return tile(i - 1 - jj, ca[0], ca[1], False)

    carry, acc = lax.fori_loop(0, i, body, (carry, acc))
    o_ref[...] = acc.astype(o_ref.dtype)


def _sb_mixer(proj, MIX, D):
    B, S, _ = proj.shape
    H = MIX // HEAD_DIM
    T = min(SB_TILE, S)
    assert T == LANES and S % T == 0
    u = _sb_consts(T)
    return pl.pallas_call(
        functools.partial(_sb_kernel, T=T, scale=1.0 / math.sqrt(HEAD_DIM)),
        grid=(B, H, S // T),
        in_specs=[pl.BlockSpec((None, T, HEAD_DIM), lambda b, h, i: (b, i, h)),
                  pl.BlockSpec((None, S, HEAD_DIM), lambda b, h, i: (b, 0, H + h)),
                  pl.BlockSpec((None, S, HEAD_DIM), lambda b, h, i: (b, 0, 2 * H + h)),
                  pl.BlockSpec(u.shape, lambda b, h, i: (0, 0))],
        out_specs=pl.BlockSpec((None, T, HEAD_DIM), lambda b, h, i: (b, i, h)),
        out_shape=jax.ShapeDtypeStruct((B, S, D), BF16),
        compiler_params=_params("parallel", "parallel", "arbitrary"),
        name="stickbreaking_mixer",
    )(proj, proj, proj, jnp.asarray(u, BF16))


def _memattn_kernel(q_ref, mk_ref, mv_ref, mixed_ref, o_ref, *, scale):
    del mixed_ref
    s = _dot_nt(q_ref[...].astype(BF16), mk_ref[...]) * scale
    e = jnp.exp(s - jnp.max(s, axis=-1, keepdims=True))
    p = e / jnp.sum(e, axis=-1, keepdims=True)
    o_ref[...] = _dot(p.astype(BF16), mv_ref[...]).astype(o_ref.dtype)


def _memattn(proj, kv, mixed, MIX):
    B, S, P = proj.shape
    D = mixed.shape[-1]
    MEMW = D - MIX
    dm = MEMW // MEM_HEADS
    NM = kv.shape[1]
    ts = _tile(S, TILE_PREFS["mem_s"], SUBLANES)
    q_off = (P - MEMW) // dm
    o_off = MIX // dm
    return pl.pallas_call(
        functools.partial(_memattn_kernel, scale=dm ** -0.5),
        grid=(B, MEM_HEADS, S // ts),
        in_specs=[pl.BlockSpec((None, ts, dm), lambda b, h, t: (b, t, q_off + h)),
                  pl.BlockSpec((None, NM, dm), lambda b, h, t: (b, 0, h)),
                  pl.BlockSpec((None, NM, dm), lambda b, h, t: (b, 0, MEM_HEADS + h)),
                  pl.BlockSpec(memory_space=pl.ANY)],
        out_specs=pl.BlockSpec((None, ts, dm), lambda b, h, t: (b, t, o_off + h)),
        out_shape=jax.ShapeDtypeStruct(mixed.shape, mixed.dtype),
        input_output_aliases={3: 0},
        compiler_params=_params("parallel", "parallel", "parallel"),
        name="memory_attention",
    )(proj, kv, kv, mixed)


def _ffn_up_kernel(h_ref, wg_ref, wv_ref, cwg_ref, cwv_ref, cbg_ref, cbv_ref, o_ref,
                   ug_ref, uv_ref, hg_ref, hv_ref, *, tm, tiles_per_seq):
    m = pl.program_id(0)
    n = pl.program_id(1)
    h = h_ref[...]
    seq_start = (m % tiles_per_seq) == 0

    def conv_branch(w_ref, cw_ref, cb_ref, u_ref, halo_ref):
        u = _dot(h, w_ref[...])
        u_ref[SUBLANES:SUBLANES + tm, :] = u
        u_ref[0:SUBLANES, :] = jnp.where(seq_start, 0.0, halo_ref[n])
        halo_ref[n] = u[tm - SUBLANES:tm, :]
        cw = cw_ref[...]
        return (cb_ref[...] + cw[2:3] * u
                + cw[1:2] * u_ref[SUBLANES - 1:SUBLANES - 1 + tm, :]
                + cw[0:1] * u_ref[SUBLANES - 2:SUBLANES - 2 + tm, :])

    gate = conv_branch(wg_ref, cwg_ref, cbg_ref, ug_ref, hg_ref)
    val = conv_branch(wv_ref, cwv_ref, cbv_ref, uv_ref, hv_ref)
    o_ref[...] = (gate * jax.nn.sigmoid(gate) * val).astype(o_ref.dtype)


def _ffn_up(h, w_up, conv_w, conv_b, S):
    T, D = h.shape
    F = w_up.shape[1] // 2
    tm = _tile(S, TILE_PREFS["up_m"], SUBLANES)
    tn = _tile(F, TILE_PREFS["up_n"], LANES)
    nt = F // tn
    assert conv_w.shape[0] == CONV_W and tm >= SUBLANES
    conv_b = conv_b.reshape(1, 2 * F).astype(F32)
    conv_w = conv_w.astype(F32)
    lo = lambda m, n: (0, n)
    hi = lambda m, n: (0, nt + n)
    return pl.pallas_call(
        functools.partial(_ffn_up_kernel, tm=tm, tiles_per_seq=S // tm),
        grid=(T // tm, nt),
        in_specs=[pl.BlockSpec((tm, D), lambda m, n: (m, 0)),
                  pl.BlockSpec((D, tn), lo), pl.BlockSpec((D, tn), hi),
                  pl.BlockSpec((CONV_W, tn), lo), pl.BlockSpec((CONV_W, tn), hi),
                  pl.BlockSpec((1, tn), lo), pl.BlockSpec((1, tn), hi)],
        out_specs=pl.BlockSpec((tm, tn), lambda m, n: (m, n)),
        out_shape=jax.ShapeDtypeStruct((T, F), BF16),
        scratch_shapes=[pltpu.VMEM((tm + SUBLANES, tn), F32), pltpu.VMEM((tm + SUBLANES, tn), F32),
                        pltpu.VMEM((nt, SUBLANES, tn), F32), pltpu.VMEM((nt, SUBLANES, tn), F32)],
        compiler_params=_params("arbitrary", "arbitrary"),
        name="ffn_up_conv_gate",
    )(h, w_up, w_up, conv_w, conv_w, conv_b, conv_b)


def kernel(x, mem, mem_norm, mem_w_kv, lb_logits, mix_norm, hgrn_w_in, hgrn_norm, hgrn_w_out, sb_w_in, sb_w_out,
           ffn_norm, ffn_w_up, ffn_conv_w, ffn_conv_b, ffn_w_down, final_norm):
    B, S, D = x.shape
    T = B * S
    depth, MIX = lb_logits.shape
    NM = mem.shape[1]

    hm = _rmsnorm(mem.reshape(B * NM, D), mem_norm, BF16)
    kv = _matmul(hm, mem_w_kv.astype(BF16), BF16, name="mem_kv_proj").reshape(B, NM, -1)

    x2 = x.reshape(T, D)
    for i in range(depth):
        j = i // 2
        h = _rmsnorm(x2, mix_norm[i], BF16)
        if i % 2 == 0:
            proj = _matmul(h, hgrn_w_in[j].astype(BF16), F32, name="hgrn_in_proj").reshape(B, S, -1)
            mixed = _hgrn_mixer(proj, lb_logits, hgrn_norm[j], i, D)
            w_out = hgrn_w_out[j]
        else:
            proj = _matmul(h, sb_w_in[j].astype(BF16), BF16, name="sb_in_proj").reshape(B, S, -1)
            mixed = _sb_mixer(proj, MIX, D)
            w_out = sb_w_out[j]
        mixed = _memattn(proj, kv, mixed, MIX)
        x2 = _matmul(mixed.reshape(T, D), w_out.astype(BF16), F32, residual=x2, name="out_proj")

        h = _rmsnorm(x2, ffn_norm[i], BF16)
        act = _ffn_up(h, ffn_w_up[i].astype(BF16), ffn_conv_w[i], ffn_conv_b[i], S)
        x2 = _matmul(act, ffn_w_down[i].astype(BF16), F32, residual=x2,
                     tm_pref=TILE_PREFS["down_m"], tn_pref=TILE_PREFS["down_n"], name="ffn_down_proj")

    return _rmsnorm(x2, final_norm, F32).reshape(B, S, D)
```

```python
import functools
import math

import numpy as np
import jax
import jax.numpy as jnp
from jax import lax
from jax.experimental import pallas as pl
from jax.experimental.pallas import tpu as pltpu

F32 = jnp.float32
BF16 = jnp.bfloat16

EPS = 1e-6
F_FLOOR = 1e-20
HEAD_DIM = 128
MEM_HEADS = 4
CONV_W = 3
LANES = 128
SUBLANES = 8
VMEM_LIMIT_BYTES = 56 * 1024 * 1024

HGRN_CHUNK = 128
SB_TILE = 512
TILE_PREFS = dict(norm_m=256, mm_m=1024, mm_n=512, up_m=1024, up_n=256, down_m=512, down_n=512,
                  hgrn_l=512, mem_s=512)


def _tile(dim, pref, quantum):
    t = min(pref, dim)
    t -= t % quantum
    while t > quantum and dim % t:
        t -= quantum
    assert t >= quantum and dim % t == 0, (dim, pref, quantum)
    return t


def _params(*sem):
    return pltpu.CompilerParams(dimension_semantics=sem, vmem_limit_bytes=VMEM_LIMIT_BYTES)


def _dot(a, b):
    return jnp.dot(a, b, preferred_element_type=F32)


def _dot_nt(a, b):
    return lax.dot_general(a, b, (((1,), (1,)), ((), ())), preferred_element_type=F32)


def _dot_tn(a, b):
    return lax.dot_general(a, b, (((0,), (0,)), ((), ())), preferred_element_type=F32)


def _rmsnorm_kernel(x_ref, g_ref, o_ref):
    x = x_ref[...]
    ms = jnp.mean(x * x, axis=-1, keepdims=True)
    o_ref[...] = (x * lax.rsqrt(ms + EPS) * g_ref[...]).astype(o_ref.dtype)


def _rmsnorm(x2d, g, out_dtype):
    T, D = x2d.shape
    tm = _tile(T, TILE_PREFS["norm_m"], SUBLANES)
    return pl.pallas_call(
        _rmsnorm_kernel,
        grid=(T // tm,),
        in_specs=[pl.BlockSpec((tm, D), lambda i: (i, 0)), pl.BlockSpec((1, D), lambda i: (0, 0))],
        out_specs=pl.BlockSpec((tm, D), lambda i: (i, 0)),
        out_shape=jax.ShapeDtypeStruct((T, D), out_dtype),
        compiler_params=_params("parallel"),
        name="rmsnorm",
    )(x2d, g.reshape(1, D).astype(F32))


def _mm_kernel(a_ref, w_ref, o_ref):
    o_ref[...] = _dot(a_ref[...], w_ref[...]).astype(o_ref.dtype)


def _mm_res_kernel(a_ref, w_ref, r_ref, o_ref):
    o_ref[...] = r_ref[...] + _dot(a_ref[...], w_ref[...])


def _matmul(a, w, out_dtype, residual=None, tm_pref=None, tn_pref=None, name="matmul"):
    M, K = a.shape
    N = w.shape[1]
    tm = _tile(M, tm_pref or TILE_PREFS["mm_m"], SUBLANES)
    tn = _tile(N, tn_pref or TILE_PREFS["mm_n"], LANES)
    in_specs = [pl.BlockSpec((tm, K), lambda m, n: (m, 0)), pl.BlockSpec((K, tn), lambda m, n: (0, n))]
    args = [a, w]
    kern = _mm_kernel
    if residual is not None:
        in_specs.append(pl.BlockSpec((tm, tn), lambda m, n: (m, n)))
        args.append(residual)
        kern = _mm_res_kernel
    return pl.pallas_call(
        kern,
        grid=(M // tm, N // tn),
        in_specs=in_specs,
        out_specs=pl.BlockSpec((tm, tn), lambda m, n: (m, n)),
        out_shape=jax.ShapeDtypeStruct((M, N), out_dtype),
        compiler_params=_params("parallel", "arbitrary"),
        name=name,
    )(*args)


def _hgrn_consts(C):
    nlev = int(math.log2(C))
    assert 1 << nlev == C
    r = np.arange(C)
    rr, jj = r[:, None], r[None, :]
    blocks = []
    for l in range(nlev):
        c = 1 << l
        ref = ((r & ~(2 * c - 1)) + c - 1)[:, None]
        odd = ((r & c) != 0)[:, None]
        blocks.append(np.where(odd, (jj > ref) & (jj <= rr), (jj > rr) & (jj <= ref)))
    blocks.append(jj <= rr)
    blocks.append(jj > rr)
    wall = np.concatenate(blocks, axis=0).astype(np.float32)
    x = rr ^ jj
    lvl = np.where(x == 0, -1, np.floor(np.log2(np.maximum(x, 1))).astype(np.int64))
    lvl = np.where(jj > rr, -2, lvl).astype(np.int32)
    return wall, lvl, nlev


def _hgrn_kernel(q_ref, zf_ref, v_ref, zg_ref, lbl_ref, gn_ref, wall_ref, lvl_ref, o_ref, st_ref,
                 *, layer, C, nlev, n_chunks):
    @pl.when(pl.program_id(2) == 0)
    def _():
        st_ref[...] = jnp.zeros_like(st_ref)

    lbl = lbl_ref[...]
    e = jnp.exp(lbl - jnp.max(lbl, axis=0, keepdims=True))
    p = e / jnp.sum(e, axis=0, keepdims=True)
    cs = p[0:1]
    for j in range(1, layer + 1):
        cs = cs + p[j:j + 1]
    lb = jnp.maximum(cs - p[0:1], 0.0)
    gn = gn_ref[...]
    lvl = lvl_ref[...]

    st = st_ref[...]
    for c in range(n_chunks):
        rows = slice(c * C, (c + 1) * C)
        zf = zf_ref[rows, :]
        q = q_ref[rows, :]
        v = v_ref[rows, :].astype(BF16)
        f = lb + (1.0 - lb) * jax.nn.sigmoid(zf)
        g = jnp.log(jnp.maximum(f, F_FLOOR))
        kin = (1.0 - lb) * jax.nn.sigmoid(-zf)
        g_hi = g.astype(BF16)
        g_lo = (g - g_hi.astype(F32)).astype(BF16)
        dd = _dot(wall_ref[...], jnp.concatenate([g_hi, g_lo], axis=1))
        d = dd[:, :HEAD_DIM] + dd[:, HEAD_DIM:]

        scores = jnp.where(lvl == -1, _dot_nt(q.astype(BF16), kin.astype(BF16)), 0.0)
        for l in range(nlev):
            el = jnp.exp(d[l * C:(l + 1) * C])
            pl_ = _dot_nt((q * el).astype(BF16), (kin * el).astype(BF16))
            scores = jnp.where(lvl == l, pl_, scores)
        b = d[nlev * C:(nlev + 1) * C]
        b_rest = d[(nlev + 1) * C:(nlev + 2) * C]

        inter = _dot_nt((q * jnp.exp(b)).astype(BF16), st.astype(BF16))
        intra = _dot(scores.astype(BF16), v)
        o = inter + intra
        st = st * jnp.exp(b[C - 1:C, :]) + _dot_tn(v, (kin * jnp.exp(b_rest)).astype(BF16))

        og = o * jax.nn.sigmoid(zg_ref[rows, :])
        ms = jnp.mean(og * og, axis=-1, keepdims=True)
        o_ref[rows, :] = (og * lax.rsqrt(ms + EPS) * gn).astype(o_ref.dtype)
    st_ref[...] = st


def _hgrn_mixer(proj, lb_logits, gnorm, layer, D):
    B, S, _ = proj.shape
    depth, MIX = lb_logits.shape
    H = MIX // HEAD_DIM
    C = min(HGRN_CHUNK, S)
    L = _tile(S, max(TILE_PREFS["hgrn_l"], C), C)
    wall, lvl, nlev = _hgrn_consts(C)
    blk = lambda off: pl.BlockSpec((None, L, HEAD_DIM), lambda b, h, c: (b, c, off + h))
    return pl.pallas_call(
        functools.partial(_hgrn_kernel, layer=layer, C=C, nlev=nlev, n_chunks=L // C),
        grid=(B, H, S // L),
        in_specs=[blk(0), blk(H), blk(2 * H), blk(3 * H),
                  pl.BlockSpec((depth, HEAD_DIM), lambda b, h, c: (0, h)),
                  pl.BlockSpec((1, HEAD_DIM), lambda b, h, c: (0, h)),
                  pl.BlockSpec(wall.shape, lambda b, h, c: (0, 0)),
                  pl.BlockSpec(lvl.shape, lambda b, h, c: (0, 0))],
        out_specs=pl.BlockSpec((None, L, HEAD_DIM), lambda b, h, c: (b, c, h)),
        out_shape=jax.ShapeDtypeStruct((B, S, D), BF16),
        scratch_shapes=[pltpu.VMEM((HEAD_DIM, HEAD_DIM), F32)],
        compiler_params=_params("parallel", "parallel", "arbitrary"),
        name="hgrn2_mixer",
    )(proj, proj, proj, proj, lb_logits.astype(F32), gnorm.reshape(1, MIX).astype(F32),
      jnp.asarray(wall, BF16), jnp.asarray(lvl))


def _sb_consts():
    r = np.arange(LANES)
    u = np.concatenate([(r[:, None] > r[None, :]), np.ones((LANES, LANES), bool)], axis=1)
    return np.concatenate([u, u], axis=0).astype(np.float32)


def _sb_kernel(q_ref, k_ref, v_ref, u_ref, o_ref, *, T, scale):
    i = pl.program_id(2)
    q = q_ref[...]
    uu = u_ref[...]

    def tile(j, carry, acc, diagonal):
        rows = pl.ds(pl.multiple_of(j * T, T), T)
        k = k_ref[rows, :]
        v = v_ref[rows, :]
        z = _dot_nt(q, k) * scale
        sp = jnp.maximum(z, 0.0) + jnp.log(1.0 + jnp.exp(-jnp.abs(z)))
        log_beta = z - sp
        if diagonal:
            strictly_earlier = (lax.broadcasted_iota(jnp.int32, (T, T), 0)
                                > lax.broadcasted_iota(jnp.int32, (T, T), 1))
            sp = jnp.where(strictly_earlier, sp, 0.0)
        hi = sp.astype(BF16)
        lo = (sp - hi.astype(F32)).astype(BF16)
        ws = [None] * (T // LANES)
        for c in reversed(range(T // LANES)):
            seg = slice(c * LANES, (c + 1) * LANES)
            cs = _dot(jnp.concatenate([hi[:, seg], lo[:, seg]], axis=1), uu)
            w = jnp.exp(log_beta[:, seg] - cs[:, :LANES] - carry)
            if diagonal:
                w = jnp.where(strictly_earlier[:, seg], w, 0.0)
            ws[c] = w.astype(BF16)
            carry = carry + cs[:, LANES:]
        return carry, acc + _dot(jnp.concatenate(ws, axis=1), v)

    carry, acc = tile(i, jnp.zeros((T, LANES), F32), jnp.zeros((T, HEAD_DIM), F32), True)

    def body(jj, ca):
        return tile(i - 1 - jj, ca[0], ca[1], False)

    carry, acc = lax.fori_loop(0, i, body, (carry, acc))
    o_ref[...] = acc.astype(o_ref.dtype)


def _sb_mixer(proj, MIX, D):
    B, S, _ = proj.shape
    H = MIX // HEAD_DIM
    T = _tile(S, SB_TILE, LANES)
    u = _sb_consts()
    return pl.pallas_call(
        functools.partial(_sb_kernel, T=T, scale=1.0 / math.sqrt(HEAD_DIM)),
        grid=(B, H, S // T),
        in_specs=[pl.BlockSpec((None, T, HEAD_DIM), lambda b, h, i: (b, i, h)),
                  pl.BlockSpec((None, S, HEAD_DIM), lambda b, h, i: (b, 0, H + h)),
                  pl.BlockSpec((None, S, HEAD_DIM), lambda b, h, i: (b, 0, 2 * H + h)),
                  pl.BlockSpec(u.shape, lambda b, h, i: (0, 0))],
        out_specs=pl.BlockSpec((None, T, HEAD_DIM), lambda b, h, i: (b, i, h)),
        out_shape=jax.ShapeDtypeStruct((B, S, D), BF16),
        compiler_params=_params("parallel", "parallel", "arbitrary"),
        name="stickbreaking_mixer",
    )(proj, proj, proj, jnp.asarray(u, BF16))


def _memattn_kernel(q_ref, mk_ref, mv_ref, mixed_ref, o_ref, *, scale):
    del mixed_ref
    s = _dot_nt(q_ref[...].astype(BF16), mk_ref[...]) * scale
    e = jnp.exp(s - jnp.max(s, axis=-1, keepdims=True))
    p = e / jnp.sum(e, axis=-1, keepdims=True)
    o_ref[...] = _dot(p.astype(BF16), mv_ref[...]).astype(o_ref.dtype)


def _memattn(proj, kv, mixed, MIX):
    B, S, P = proj.shape
    D = mixed.shape[-1]
    MEMW = D - MIX
    dm = MEMW // MEM_HEADS
    NM = kv.shape[1]
    ts = _tile(S, TILE_PREFS["mem_s"], SUBLANES)
    q_off = (P - MEMW) // dm
    o_off = MIX // dm
    return pl.pallas_call(
        functools.partial(_memattn_kernel, scale=dm ** -0.5),
        grid=(B, MEM_HEADS, S // ts),
        in_specs=[pl.BlockSpec((None, ts, dm), lambda b, h, t: (b, t, q_off + h)),
                  pl.BlockSpec((None, NM, dm), lambda b, h, t: (b, 0, h)),
                  pl.BlockSpec((None, NM, dm), lambda b, h, t: (b, 0, MEM_HEADS + h)),
                  pl.BlockSpec(memory_space=pl.ANY)],
        out_specs=pl.BlockSpec((None, ts, dm), lambda b, h, t: (b, t, o_off + h)),
        out_shape=jax.ShapeDtypeStruct(mixed.shape, mixed.dtype),
        input_output_aliases={3: 0},
        compiler_params=_params("parallel", "parallel", "parallel"),
        name="memory_attention",
    )(proj, kv, kv, mixed)


def _ffn_up_kernel(h_ref, wg_ref, wv_ref, cwg_ref, cwv_ref, cbg_ref, cbv_ref, o_ref, hg_ref, hv_ref,
                   *, tm, tiles_per_seq):
    m = pl.program_id(0)
    n = pl.program_id(1)
    tn = o_ref.shape[1]
    h = h_ref[...]
    seq_start = (m % tiles_per_seq) == 0
    sub = lax.broadcasted_iota(jnp.int32, (tm // SUBLANES, SUBLANES, tn), 1)

    def shifted(u3, prev, k):
        rot = pltpu.roll(jnp.concatenate([prev[None], u3], axis=0), k, 1)
        return jnp.where(sub < k, rot[:-1], rot[1:])

    def conv_branch(w_ref, cw_ref, cb_ref, halo_ref):
        u = _dot(h, w_ref[...])
        prev = jnp.where(seq_start, 0.0, halo_ref[n])
        halo_ref[n] = u[tm - SUBLANES:tm, :]
        cw = cw_ref[...]
        u3 = u.reshape(tm // SUBLANES, SUBLANES, tn)
        c = cb_ref[...] + cw[2:3] * u3 + cw[1:2] * shifted(u3, prev, 1) + cw[0:1] * shifted(u3, prev, 2)
        return c.reshape(tm, tn)

    gate = conv_branch(wg_ref, cwg_ref, cbg_ref, hg_ref)
    val = conv_branch(wv_ref, cwv_ref, cbv_ref, hv_ref)
    o_ref[...] = (gate * jax.nn.sigmoid(gate) * val).astype(o_ref.dtype)


def _ffn_up(h, w_up, conv_w, conv_b, S):
    T, D = h.shape
    F = w_up.shape[1] // 2
    tm = _tile(S, TILE_PREFS["up_m"], SUBLANES)
    tn = _tile(F, TILE_PREFS["up_n"], LANES)
    nt = F // tn
    assert conv_w.shape[0] == CONV_W
    conv_b = conv_b.reshape(1, 2 * F).astype(F32)
    conv_w = conv_w.astype(F32)
    lo = lambda m, n: (0, n)
    hi = lambda m, n: (0, nt + n)
    return pl.pallas_call(
        functools.partial(_ffn_up_kernel, tm=tm, tiles_per_seq=S // tm),
        grid=(T // tm, nt),
        in_specs=[pl.BlockSpec((tm, D), lambda m, n: (m, 0)),
                  pl.BlockSpec((D, tn), lo), pl.BlockSpec((D, tn), hi),
                  pl.BlockSpec((CONV_W, tn), lo), pl.BlockSpec((CONV_W, tn), hi),
                  pl.BlockSpec((1, tn), lo), pl.BlockSpec((1, tn), hi)],
        out_specs=pl.BlockSpec((tm, tn), lambda m, n: (m, n)),
        out_shape=jax.ShapeDtypeStruct((T, F), BF16),
        scratch_shapes=[pltpu.VMEM((nt, SUBLANES, tn), F32), pltpu.VMEM((nt, SUBLANES, tn), F32)],
        compiler_params=_params("arbitrary", "arbitrary"),
        name="ffn_up_conv_gate",
    )(h, w_up, w_up, conv_w, conv_w, conv_b, conv_b)


def kernel(x, mem, mem_norm, mem_w_kv, lb_logits, mix_norm, hgrn_w_in, hgrn_norm, hgrn_w_out, sb_w_in, sb_w_out,
           ffn_norm, ffn_w_up, ffn_conv_w, ffn_conv_b, ffn_w_down, final_norm):
    B, S, D = x.shape
    T = B * S
    depth, MIX = lb_logits.shape
    NM = mem.shape[1]

    hm = _rmsnorm(mem.reshape(B * NM, D), mem_norm, BF16)
    kv = _matmul(hm, mem_w_kv.astype(BF16), BF16, name="mem_kv_proj").reshape(B, NM, -1)

    x2 = x.reshape(T, D)
    for i in range(depth):
        j = i // 2
        h = _rmsnorm(x2, mix_norm[i], BF16)
        if i % 2 == 0:
            proj = _matmul(h, hgrn_w_in[j].astype(BF16), F32, name="hgrn_in_proj").reshape(B, S, -1)
            mixed = _hgrn_mixer(proj, lb_logits, hgrn_norm[j], i, D)
            w_out = hgrn_w_out[j]
        else:
            proj = _matmul(h, sb_w_in[j].astype(BF16), BF16, name="sb_in_proj").reshape(B, S, -1)
            mixed = _sb_mixer(proj, MIX, D)
            w_out = sb_w_out[j]
        mixed = _memattn(proj, kv, mixed, MIX)
        x2 = _matmul(mixed.reshape(T, D), w_out.astype(BF16), F32, residual=x2, name="out_proj")

        h = _rmsnorm(x2, ffn_norm[i], BF16)
        act = _ffn_up(h, ffn_w_up[i].astype(BF16), ffn_conv_w[i], ffn_conv_b[i], S)
        x2 = _matmul(act, ffn_w_down[i].astype(BF16), F32, residual=x2,
                     tm_pref=TILE_PREFS["down_m"], tn_pref=TILE_PREFS["down_n"], name="ffn_down_proj")

    return _rmsnorm(x2, final_norm, F32).reshape(B, S, D)
```

```python
import functools
import math

import numpy as np
import jax
import jax.numpy as jnp
from jax import lax
from jax.experimental import pallas as pl
from jax.experimental.pallas import tpu as pltpu

F32 = jnp.float32
BF16 = jnp.bfloat16

EPS = 1e-6
F_FLOOR = 1e-20
LOG2E = 1.4426950408889634
HEAD_DIM = 128
MEM_HEADS = 4
CONV_W = 3
LANES = 128
SUBLANES = 8
VMEM_LIMIT_BYTES = 56 * 1024 * 1024

HGRN_CHUNK = 128
SB_TILE = 512
SB_HEADS_PER_STEP = 2
TILE_PREFS = dict(norm_m=256, mm_m=1024, mm_n=512, up_m=1024, up_n=256, down_m=512, down_n=512,
                  hgrn_l=512, mem_s=512)


def _tile(dim, pref, quantum):
    t = min(pref, dim)
    t -= t % quantum
    while t > quantum and dim % t:
        t -= quantum
    assert t >= quantum and dim % t == 0, (dim, pref, quantum)
    return t


def _params(*sem):
    return pltpu.CompilerParams(dimension_semantics=sem, vmem_limit_bytes=VMEM_LIMIT_BYTES)


def _dot(a, b):
    return jnp.dot(a, b, preferred_element_type=F32)


def _dot_nt(a, b):
    return lax.dot_general(a, b, (((1,), (1,)), ((), ())), preferred_element_type=F32)


def _dot_tn(a, b):
    return lax.dot_general(a, b, (((0,), (0,)), ((), ())), preferred_element_type=F32)


def _rmsnorm_kernel(x_ref, g_ref, o_ref):
    x = x_ref[...]
    ms = jnp.mean(x * x, axis=-1, keepdims=True)
    o_ref[...] = (x * lax.rsqrt(ms + EPS) * g_ref[...]).astype(o_ref.dtype)


def _rmsnorm(x2d, g, out_dtype):
    T, D = x2d.shape
    tm = _tile(T, TILE_PREFS["norm_m"], SUBLANES)
    return pl.pallas_call(
        _rmsnorm_kernel,
        grid=(T // tm,),
        in_specs=[pl.BlockSpec((tm, D), lambda i: (i, 0)), pl.BlockSpec((1, D), lambda i: (0, 0))],
        out_specs=pl.BlockSpec((tm, D), lambda i: (i, 0)),
        out_shape=jax.ShapeDtypeStruct((T, D), out_dtype),
        compiler_params=_params("parallel"),
        name="rmsnorm",
    )(x2d, g.reshape(1, D).astype(F32))


def _mm_kernel(a_ref, w_ref, o_ref):
    o_ref[...] = _dot(a_ref[...], w_ref[...].astype(BF16)).astype(o_ref.dtype)


def _mm_scale_kernel(a_ref, w_ref, s_ref, o_ref):
    o_ref[...] = (_dot(a_ref[...], w_ref[...].astype(BF16)) * s_ref[...]).astype(o_ref.dtype)


def _mm_res_kernel(a_ref, w_ref, r_ref, o_ref):
    o_ref[...] = r_ref[...] + _dot(a_ref[...], w_ref[...].astype(BF16))


def _matmul(a, w, out_dtype, layer=None, residual=None, col_scale=None, tm_pref=None, tn_pref=None, name="matmul"):
    M, K = a.shape
    N = w.shape[-1]
    tm = _tile(M, tm_pref or TILE_PREFS["mm_m"], SUBLANES)
    tn = _tile(N, tn_pref or TILE_PREFS["mm_n"], LANES)
    if layer is None:
        w_spec = pl.BlockSpec((K, tn), lambda m, n: (0, n))
    else:
        w_spec = pl.BlockSpec((None, K, tn), lambda m, n: (layer, 0, n))
    in_specs = [pl.BlockSpec((tm, K), lambda m, n: (m, 0)), w_spec]
    args = [a, w]
    kern = _mm_kernel
    assert residual is None or col_scale is None
    if residual is not None:
        in_specs.append(pl.BlockSpec((tm, tn), lambda m, n: (m, n)))
        args.append(residual)
        kern = _mm_res_kernel
    if col_scale is not None:
        in_specs.append(pl.BlockSpec((1, tn), lambda m, n: (0, n)))
        args.append(col_scale.reshape(1, N).astype(F32))
        kern = _mm_scale_kernel
    return pl.pallas_call(
        kern,
        grid=(M // tm, N // tn),
        in_specs=in_specs,
        out_specs=pl.BlockSpec((tm, tn), lambda m, n: (m, n)),
        out_shape=jax.ShapeDtypeStruct((M, N), out_dtype),
        compiler_params=_params("parallel", "arbitrary"),
        name=name,
    )(*args)


def _hgrn_consts(C):
    nlev = int(math.log2(C))
    assert 1 << nlev == C
    r = np.arange(C)
    rr, jj = r[:, None], r[None, :]
    blocks = []
    for l in range(nlev):
        c = 1 << l
        ref = ((r & ~(2 * c - 1)) + c - 1)[:, None]
        odd = ((r & c) != 0)[:, None]
        blocks.append(np.where(odd, (jj > ref) & (jj <= rr), (jj > rr) & (jj <= ref)))
    blocks.append(jj <= rr)
    blocks.append(jj > rr)
    wall = np.concatenate(blocks, axis=0).astype(np.float32)
    x = rr ^ jj
    lvl = np.where(x == 0, -1, np.floor(np.log2(np.maximum(x, 1))).astype(np.int64))
    lvl = np.where(jj > rr, -2, lvl).astype(np.int32)
    return wall, lvl, nlev


def _hgrn_kernel(q_ref, zf_ref, v_ref, zg_ref, lbl_ref, gn_ref, wall_ref, lvl_ref, o_ref, st_ref,
                 *, layer, C, nlev, n_chunks):
    @pl.when(pl.program_id(2) == 0)
    def _():
        st_ref[...] = jnp.zeros_like(st_ref)

    lbl = lbl_ref[...]
    e = jnp.exp(lbl - jnp.max(lbl, axis=0, keepdims=True))
    p = e / jnp.sum(e, axis=0, keepdims=True)
    cs = p[0:1]
    for j in range(1, layer + 1):
        cs = cs + p[j:j + 1]
    lb = jnp.maximum(cs - p[0:1], 0.0)
    gn = gn_ref[...]
    lvl = lvl_ref[...]

    st = st_ref[...]
    for c in range(n_chunks):
        rows = slice(c * C, (c + 1) * C)
        zf = zf_ref[rows, :]
        q = q_ref[rows, :]
        v = v_ref[rows, :].astype(BF16)
        f = lb + (1.0 - lb) * jax.nn.sigmoid(zf)
        g = jnp.log(jnp.maximum(f, F_FLOOR))
        kin = (1.0 - lb) * jax.nn.sigmoid(-zf)
        g_hi = g.astype(BF16)
        g_lo = (g - g_hi.astype(F32)).astype(BF16)
        dd = _dot(wall_ref[...], jnp.concatenate([g_hi, g_lo], axis=1))
        d = dd[:, :HEAD_DIM] + dd[:, HEAD_DIM:]

        scores = jnp.where(lvl == -1, _dot_nt(q.astype(BF16), kin.astype(BF16)), 0.0)
        for l in range(nlev):
            el = jnp.exp(d[l * C:(l + 1) * C])
            pl_ = _dot_nt((q * el).astype(BF16), (kin * el).astype(BF16))
            scores = jnp.where(lvl == l, pl_, scores)
        b = d[nlev * C:(nlev + 1) * C]
        b_rest = d[(nlev + 1) * C:(nlev + 2) * C]

        inter = _dot_nt((q * jnp.exp(b)).astype(BF16), st.astype(BF16))
        intra = _dot(scores.astype(BF16), v)
        o = inter + intra
        st = st * jnp.exp(b[C - 1:C, :]) + _dot_tn(v, (kin * jnp.exp(b_rest)).astype(BF16))

        og = o * jax.nn.sigmoid(zg_ref[rows, :])
        ms = jnp.mean(og * og, axis=-1, keepdims=True)
        o_ref[rows, :] = (og * lax.rsqrt(ms + EPS) * gn).astype(o_ref.dtype)
    st_ref[...] = st


def _hgrn_mixer(proj, lb_logits, gnorm, layer, D):
    B, S, _ = proj.shape
    depth, MIX = lb_logits.shape
    H = MIX // HEAD_DIM
    C = min(HGRN_CHUNK, S)
    L = _tile(S, max(TILE_PREFS["hgrn_l"], C), C)
    wall, lvl, nlev = _hgrn_consts(C)
    blk = lambda off: pl.BlockSpec((None, L, HEAD_DIM), lambda b, h, c: (b, c, off + h))
    return pl.pallas_call(
        functools.partial(_hgrn_kernel, layer=layer, C=C, nlev=nlev, n_chunks=L // C),
        grid=(B, H, S // L),
        in_specs=[blk(0), blk(H), blk(2 * H), blk(3 * H),
                  pl.BlockSpec((depth, HEAD_DIM), lambda b, h, c: (0, h)),
                  pl.BlockSpec((1, HEAD_DIM), lambda b, h, c: (0, h)),
                  pl.BlockSpec(wall.shape, lambda b, h, c: (0, 0)),
                  pl.BlockSpec(lvl.shape, lambda b, h, c: (0, 0))],
        out_specs=pl.BlockSpec((None, L, HEAD_DIM), lambda b, h, c: (b, c, h)),
        out_shape=jax.ShapeDtypeStruct((B, S, D), BF16),
        scratch_shapes=[pltpu.VMEM((HEAD_DIM, HEAD_DIM), F32)],
        compiler_params=_params("parallel", "parallel", "arbitrary"),
        name="hgrn2_mixer",
    )(proj, proj, proj, proj, lb_logits.astype(F32), gnorm.reshape(1, MIX).astype(F32),
      jnp.asarray(wall, BF16), jnp.asarray(lvl))


def _sb_consts():
    r = np.arange(LANES)
    u = np.concatenate([(r[:, None] > r[None, :]), np.ones((LANES, LANES), bool)], axis=1)
    return np.concatenate([u, u], axis=0).astype(np.float32)


def _sb_kernel(q_ref, k_ref, v_ref, u_ref, o_ref, z_ref, w_ref, *, T, G):
    i = pl.program_id(2)
    uu = u_ref[...]

    def rows_of(j):
        return slice(j * T, (j + 1) * T) if isinstance(j, int) else pl.ds(pl.multiple_of(j * T, T), T)

    def lanes_of(g):
        return slice(g * HEAD_DIM, (g + 1) * HEAD_DIM)

    def scores(g, j):
        return _dot_nt(q_ref[:, lanes_of(g)], k_ref[rows_of(j), lanes_of(g)])

    def weighted_values(g, w, j):
        return _dot(w, v_ref[rows_of(j), lanes_of(g)])

    def weights(z, carry, diagonal):
        neg_abs = lax.bitcast_convert_type(lax.bitcast_convert_type(z, jnp.uint32) | jnp.uint32(0x80000000), F32)
        sp = jnp.maximum(z, 0.0) + jnp.log(1.0 + jnp.exp2(neg_abs)) * LOG2E
        log_beta = z - sp
        if diagonal:
            strictly_earlier = (lax.broadcasted_iota(jnp.int32, (T, T), 0)
                                > lax.broadcasted_iota(jnp.int32, (T, T), 1))
            sp = jnp.where(strictly_earlier, sp, 0.0)
        hi = sp.astype(BF16)
        lo = (sp - hi.astype(F32)).astype(BF16)
        ws = [None] * (T // LANES)
        for c in reversed(range(T // LANES)):
            seg = slice(c * LANES, (c + 1) * LANES)
            cs = _dot(jnp.concatenate([hi[:, seg], lo[:, seg]], axis=1), uu)
            w = jnp.exp2(log_beta[:, seg] - cs[:, :LANES] - carry)
            if diagonal:
                w = jnp.where(strictly_earlier[:, seg], w, 0.0)
            ws[c] = w.astype(BF16)
            carry = carry + cs[:, LANES:]
        return jnp.concatenate(ws, axis=1), carry

    def first(g):
        w_ref[g], carry = weights(scores(g, i), jnp.zeros((T, LANES), F32), True)
        z_ref[g] = scores(g, jnp.maximum(i - 1, 0))
        return carry, jnp.zeros((T, HEAD_DIM), F32)

    def step(g, j, carry, acc):
        acc = acc + weighted_values(g, w_ref[g], j + 1)
        w, carry = weights(z_ref[g], carry, False)
        z_ref[g] = scores(g, jnp.maximum(j - 1, 0))
        w_ref[g] = w
        return carry, acc

    state = tuple(first(g) for g in range(G))
    state = lax.fori_loop(0, i, lambda jj, st: tuple(step(g, i - 1 - jj, *st[g]) for g in range(G)), state)
    for g in range(G):
        o_ref[:, lanes_of(g)] = (state[g][1] + weighted_values(g, w_ref[g], 0)).astype(o_ref.dtype)


def _sb_mixer(proj, MIX, D):
    B, S, _ = proj.shape
    H = MIX // HEAD_DIM
    T = _tile(S, SB_TILE, LANES)
    G = SB_HEADS_PER_STEP
    assert H % G == 0
    HG = H // G
    W = G * HEAD_DIM
    u = _sb_consts()
    return pl.pallas_call(
        functools.partial(_sb_kernel, T=T, G=G),
        grid=(B, HG, S // T),
        in_specs=[pl.BlockSpec((None, T, W), lambda b, h, i: (b, i, h)),
                  pl.BlockSpec((None, S, W), lambda b, h, i: (b, 0, HG + h)),
                  pl.BlockSpec((None, S, W), lambda b, h, i: (b, 0, 2 * HG + h)),
                  pl.BlockSpec(u.shape, lambda b, h, i: (0, 0))],
        out_specs=pl.BlockSpec((None, T, W), lambda b, h, i: (b, i, h)),
        out_shape=jax.ShapeDtypeStruct((B, S, D), BF16),
        scratch_shapes=[pltpu.VMEM((G, T, T), F32), pltpu.VMEM((G, T, T), BF16)],
        compiler_params=_params("parallel", "parallel", "arbitrary"),
        name="stickbreaking_mixer",
    )(proj, proj, proj, jnp.asarray(u, BF16))


def _memattn_kernel(q_ref, mk_ref, mv_ref, mixed_ref, o_ref, *, scale):
    del mixed_ref
    s = _dot_nt(q_ref[...].astype(BF16), mk_ref[...]) * scale
    e = jnp.exp(s - jnp.max(s, axis=-1, keepdims=True))
    p = e / jnp.sum(e, axis=-1, keepdims=True)
    o_ref[...] = _dot(p.astype(BF16), mv_ref[...]).astype(o_ref.dtype)


def _memattn(proj, kv, mixed, MIX):
    B, S, P = proj.shape
    D = mixed.shape[-1]
    MEMW = D - MIX
    dm = MEMW // MEM_HEADS
    NM = kv.shape[1]
    ts = _tile(S, TILE_PREFS["mem_s"], SUBLANES)
    q_off = (P - MEMW) // dm
    o_off = MIX // dm
    return pl.pallas_call(
        functools.partial(_memattn_kernel, scale=dm ** -0.5),
        grid=(B, MEM_HEADS, S // ts),
        in_specs=[pl.BlockSpec((None, ts, dm), lambda b, h, t: (b, t, q_off + h)),
                  pl.BlockSpec((None, NM, dm), lambda b, h, t: (b, 0, h)),
                  pl.BlockSpec((None, NM, dm), lambda b, h, t: (b, 0, MEM_HEADS + h)),
                  pl.BlockSpec(memory_space=pl.ANY)],
        out_specs=pl.BlockSpec((None, ts, dm), lambda b, h, t: (b, t, o_off + h)),
        out_shape=jax.ShapeDtypeStruct(mixed.shape, mixed.dtype),
        input_output_aliases={3: 0},
        compiler_params=_params("parallel", "parallel", "parallel"),
        name="memory_attention",
    )(proj, kv, kv, mixed)


def _ffn_up_kernel(h_ref, wg_ref, wv_ref, cwg_ref, cwv_ref, cbg_ref, cbv_ref, o_ref, ug_ref, uv_ref, hg_ref, hv_ref,
                   *, tm, tiles_per_seq):
    m = pl.program_id(0)
    n = pl.program_id(1)
    h = h_ref[...]
    seq_start = (m % tiles_per_seq) == 0

    def conv_branch(w_ref, cw_ref, cb_ref, u_ref, halo_ref):
        u = _dot(h, w_ref[...].astype(BF16))
        u_ref[SUBLANES:SUBLANES + tm, :] = u
        u_ref[0:SUBLANES, :] = jnp.where(seq_start, 0.0, halo_ref[n])
        halo_ref[n] = u[tm - SUBLANES:tm, :]
        cw = cw_ref[...]
        return (cb_ref[...] + cw[2:3] * u
                + cw[1:2] * u_ref[SUBLANES - 1:SUBLANES - 1 + tm, :]
                + cw[0:1] * u_ref[SUBLANES - 2:SUBLANES - 2 + tm, :])

    gate = conv_branch(wg_ref, cwg_ref, cbg_ref, ug_ref, hg_ref)
    val = conv_branch(wv_ref, cwv_ref, cbv_ref, uv_ref, hv_ref)
    o_ref[...] = (gate * jax.nn.sigmoid(gate) * val).astype(o_ref.dtype)


def _ffn_up(h, w_up, conv_w, conv_b, layer, S):
    T, D = h.shape
    F = w_up.shape[-1] // 2
    tm = _tile(S, TILE_PREFS["up_m"], SUBLANES)
    tn = _tile(F, TILE_PREFS["up_n"], LANES)
    nt = F // tn
    assert conv_w.shape[1] == CONV_W and conv_b.shape[1] == 1
    lo = lambda m, n: (layer, 0, n)
    hi = lambda m, n: (layer, 0, nt + n)
    return pl.pallas_call(
        functools.partial(_ffn_up_kernel, tm=tm, tiles_per_seq=S // tm),
        grid=(T // tm, nt),
        in_specs=[pl.BlockSpec((tm, D), lambda m, n: (m, 0)),
                  pl.BlockSpec((None, D, tn), lo), pl.BlockSpec((None, D, tn), hi),
                  pl.BlockSpec((None, CONV_W, tn), lo), pl.BlockSpec((None, CONV_W, tn), hi),
                  pl.BlockSpec((None, 1, tn), lo), pl.BlockSpec((None, 1, tn), hi)],
        out_specs=pl.BlockSpec((tm, tn), lambda m, n: (m, n)),
        out_shape=jax.ShapeDtypeStruct((T, F), BF16),
        scratch_shapes=[pltpu.VMEM((tm + SUBLANES, tn), F32), pltpu.VMEM((tm + SUBLANES, tn), F32),
                        pltpu.VMEM((nt, SUBLANES, tn), F32), pltpu.VMEM((nt, SUBLANES, tn), F32)],
        compiler_params=_params("arbitrary", "arbitrary"),
        name="ffn_up_conv_gate",
    )(h, w_up, w_up, conv_w, conv_w, conv_b, conv_b)


def kernel(x, mem, mem_norm, mem_w_kv, lb_logits, mix_norm, hgrn_w_in, hgrn_norm, hgrn_w_out, sb_w_in, sb_w_out,
           ffn_norm, ffn_w_up, ffn_conv_w, ffn_conv_b, ffn_w_down, final_norm):
    B, S, D = x.shape
    T = B * S
    depth, MIX = lb_logits.shape
    NM = mem.shape[1]

    hm = _rmsnorm(mem.reshape(B * NM, D), mem_norm, BF16)
    kv = _matmul(hm, mem_w_kv, BF16, name="mem_kv_proj").reshape(B, NM, -1)

    sb_cols = sb_w_in.shape[-1]
    sb_scale = jnp.where(jnp.arange(sb_cols) < MIX, LOG2E / math.sqrt(HEAD_DIM), 1.0).astype(F32)
    w_down = ffn_w_down.astype(BF16)
    conv_b = ffn_conv_b.reshape(depth, 1, -1).astype(F32)
    conv_w = ffn_conv_w.astype(F32)

    x2 = x.reshape(T, D)
    for i in range(depth):
        j = i // 2
        h = _rmsnorm(x2, mix_norm[i], BF16)
        if i % 2 == 0:
            proj = _matmul(h, hgrn_w_in, F32, layer=j, name="hgrn_in_proj").reshape(B, S, -1)
            mixed = _hgrn_mixer(proj, lb_logits, hgrn_norm[j], i, D)
            w_out = hgrn_w_out
        else:
            proj = _matmul(h, sb_w_in, BF16, layer=j, col_scale=sb_scale, name="sb_in_proj").reshape(B, S, -1)
            mixed = _sb_mixer(proj, MIX, D)
            w_out = sb_w_out
        mixed = _memattn(proj, kv, mixed, MIX)
        x2 = _matmul(mixed.reshape(T, D), w_out, F32, layer=j, residual=x2, name="out_proj")

        h = _rmsnorm(x2, ffn_norm[i], BF16)
        act = _ffn_up(h, ffn_w_up, conv_w, conv_b, i, S)
        x2 = _matmul(act, w_down, F32, layer=i, residual=x2,
                     tm_pref=TILE_PREFS["down_m"], tn_pref=TILE_PREFS["down_n"], name="ffn_down_proj")

    return _rmsnorm(x2, final_norm, F32).reshape(B, S, D)
```

```python
import functools
import math

import numpy as np
import jax
import jax.numpy as jnp
from jax import lax
from jax.experimental import pallas as pl
from jax.experimental.pallas import tpu as pltpu

F32 = jnp.float32
BF16 = jnp.bfloat16

EPS = 1e-6
F_FLOOR = 1e-20
LOG2E = 1.4426950408889634
HEAD_DIM = 128
MEM_HEADS = 4
CONV_W = 3
LANES = 128
SUBLANES = 8
VMEM_LIMIT_BYTES = 56 * 1024 * 1024

HGRN_CHUNK = 128
SB_TILE = 512
SB_HEADS_PER_STEP = 3
TILE_PREFS = dict(norm_m=256, mm_m=1024, mm_n=512, up_m=1024, up_n=256, down_m=512, down_n=512,
                  hgrn_l=512, mem_s=512)


def _tile(dim, pref, quantum):
    t = min(pref, dim)
    t -= t % quantum
    while t > quantum and dim % t:
        t -= quantum
    assert t >= quantum and dim % t == 0, (dim, pref, quantum)
    return t


def _params(*sem):
    return pltpu.CompilerParams(dimension_semantics=sem, vmem_limit_bytes=VMEM_LIMIT_BYTES)


def _dot(a, b):
    return jnp.dot(a, b, preferred_element_type=F32)


def _dot_nt(a, b):
    return lax.dot_general(a, b, (((1,), (1,)), ((), ())), preferred_element_type=F32)


def _dot_tn(a, b):
    return lax.dot_general(a, b, (((0,), (0,)), ((), ())), preferred_element_type=F32)


def _rmsnorm_kernel(x_ref, g_ref, o_ref):
    x = x_ref[...]
    ms = jnp.mean(x * x, axis=-1, keepdims=True)
    o_ref[...] = (x * lax.rsqrt(ms + EPS) * g_ref[...]).astype(o_ref.dtype)


def _rmsnorm(x2d, g, out_dtype):
    T, D = x2d.shape
    tm = _tile(T, TILE_PREFS["norm_m"], SUBLANES)
    return pl.pallas_call(
        _rmsnorm_kernel,
        grid=(T // tm,),
        in_specs=[pl.BlockSpec((tm, D), lambda i: (i, 0)), pl.BlockSpec((1, D), lambda i: (0, 0))],
        out_specs=pl.BlockSpec((tm, D), lambda i: (i, 0)),
        out_shape=jax.ShapeDtypeStruct((T, D), out_dtype),
        compiler_params=_params("parallel"),
        name="rmsnorm",
    )(x2d, g.reshape(1, D).astype(F32))


def _mm_kernel(a_ref, w_ref, o_ref):
    o_ref[...] = _dot(a_ref[...], w_ref[...].astype(BF16)).astype(o_ref.dtype)


def _mm_scale_kernel(a_ref, w_ref, s_ref, o_ref):
    o_ref[...] = (_dot(a_ref[...], w_ref[...].astype(BF16)) * s_ref[...]).astype(o_ref.dtype)


def _mm_res_kernel(a_ref, w_ref, r_ref, o_ref):
    o_ref[...] = r_ref[...] + _dot(a_ref[...], w_ref[...].astype(BF16))


def _matmul(a, w, out_dtype, layer=None, residual=None, col_scale=None, tm_pref=None, tn_pref=None, name="matmul"):
    M, K = a.shape
    N = w.shape[-1]
    tm = _tile(M, tm_pref or TILE_PREFS["mm_m"], SUBLANES)
    tn = _tile(N, tn_pref or TILE_PREFS["mm_n"], LANES)
    if layer is None:
        w_spec = pl.BlockSpec((K, tn), lambda m, n: (0, n))
    else:
        w_spec = pl.BlockSpec((None, K, tn), lambda m, n: (layer, 0, n))
    in_specs = [pl.BlockSpec((tm, K), lambda m, n: (m, 0)), w_spec]
    args = [a, w]
    kern = _mm_kernel
    assert residual is None or col_scale is None
    if residual is not None:
        in_specs.append(pl.BlockSpec((tm, tn), lambda m, n: (m, n)))
        args.append(residual)
        kern = _mm_res_kernel
    if col_scale is not None:
        in_specs.append(pl.BlockSpec((1, tn), lambda m, n: (0, n)))
        args.append(col_scale.reshape(1, N).astype(F32))
        kern = _mm_scale_kernel
    return pl.pallas_call(
        kern,
        grid=(M // tm, N // tn),
        in_specs=in_specs,
        out_specs=pl.BlockSpec((tm, tn), lambda m, n: (m, n)),
        out_shape=jax.ShapeDtypeStruct((M, N), out_dtype),
        compiler_params=_params("parallel", "arbitrary"),
        name=name,
    )(*args)


def _hgrn_consts(C):
    nlev = int(math.log2(C))
    assert 1 << nlev == C
    r = np.arange(C)
    rr, jj = r[:, None], r[None, :]
    blocks = []
    for l in range(nlev):
        c = 1 << l
        ref = ((r & ~(2 * c - 1)) + c - 1)[:, None]
        odd = ((r & c) != 0)[:, None]
        blocks.append(np.where(odd, (jj > ref) & (jj <= rr), (jj > rr) & (jj <= ref)))
    blocks.append(jj <= rr)
    blocks.append(jj > rr)
    wall = np.concatenate(blocks, axis=0).astype(np.float32)
    x = rr ^ jj
    lvl = np.where(x == 0, -1, np.floor(np.log2(np.maximum(x, 1))).astype(np.int64))
    lvl = np.where(jj > rr, -2, lvl).astype(np.int32)
    return wall, lvl, nlev


def _hgrn_kernel(q_ref, zf_ref, v_ref, zg_ref, lbl_ref, gn_ref, wall_ref, lvl_ref, o_ref, st_ref,
                 *, layer, C, nlev, n_chunks):
    @pl.when(pl.program_id(2) == 0)
    def _():
        st_ref[...] = jnp.zeros_like(st_ref)

    lbl = lbl_ref[...]
    e = jnp.exp(lbl - jnp.max(lbl, axis=0, keepdims=True))
    p = e / jnp.sum(e, axis=0, keepdims=True)
    cs = p[0:1]
    for j in range(1, layer + 1):
        cs = cs + p[j:j + 1]
    lb = jnp.maximum(cs - p[0:1], 0.0)
    gn = gn_ref[...]
    lvl = lvl_ref[...]

    sts = [st_ref[0], st_ref[1]]
    for c in range(n_chunks):
        rows = slice(c * C, (c + 1) * C)
        zf = zf_ref[rows, :]
        f = lb + (1.0 - lb) * jax.nn.sigmoid(zf)
        g = jnp.log(jnp.maximum(f, F_FLOOR))
        kin2 = (1.0 - lb) * jax.nn.sigmoid(-zf)
        g_hi = g.astype(BF16)
        g_lo = (g - g_hi.astype(F32)).astype(BF16)
        d2 = _dot(wall_ref[...], jnp.concatenate([g_hi, g_lo], axis=0))
        og2 = []
        for hd in range(2):
            head = slice(hd * HEAD_DIM, (hd + 1) * HEAD_DIM)
            d = d2[:, head]
            q = q_ref[rows, head]
            v = v_ref[rows, head].astype(BF16)
            kin = kin2[:, head]
            scores = jnp.where(lvl == -1, _dot_nt(q.astype(BF16), kin.astype(BF16)), 0.0)
            for l in range(nlev):
                el = jnp.exp(d[l * C:(l + 1) * C])
                pl_ = _dot_nt((q * el).astype(BF16), (kin * el).astype(BF16))
                scores = jnp.where(lvl == l, pl_, scores)
            b = d[nlev * C:(nlev + 1) * C]
            b_rest = d[(nlev + 1) * C:(nlev + 2) * C]

            st = sts[hd]
            inter = _dot_nt((q * jnp.exp(b)).astype(BF16), st.astype(BF16))
            intra = _dot(scores.astype(BF16), v)
            sts[hd] = st * jnp.exp(b[C - 1:C, :]) + _dot_tn(v, (kin * jnp.exp(b_rest)).astype(BF16))

            og = (inter + intra) * jax.nn.sigmoid(zg_ref[rows, head])
            ms = jnp.mean(og * og, axis=-1, keepdims=True)
            og2.append(og * lax.rsqrt(ms + EPS))
        o_ref[rows, :] = (jnp.concatenate(og2, axis=1) * gn).astype(o_ref.dtype)
    st_ref[0] = sts[0]
    st_ref[1] = sts[1]


def _hgrn_mixer(proj, lb_logits, gnorm, layer, D):
    B, S, _ = proj.shape
    depth, MIX = lb_logits.shape
    W = 2 * HEAD_DIM
    HP = MIX // W
    assert MIX % W == 0
    C = min(HGRN_CHUNK, S)
    L = _tile(S, max(TILE_PREFS["hgrn_l"], C), C)
    wall, lvl, nlev = _hgrn_consts(C)
    wall = np.concatenate([wall, wall], axis=1)
    blk = lambda off: pl.BlockSpec((None, L, W), lambda b, h, c: (b, c, off + h))
    return pl.pallas_call(
        functools.partial(_hgrn_kernel, layer=layer, C=C, nlev=nlev, n_chunks=L // C),
        grid=(B, HP, S // L),
        in_specs=[blk(0), blk(HP), blk(2 * HP), blk(3 * HP),
                  pl.BlockSpec((depth, W), lambda b, h, c: (0, h)),
                  pl.BlockSpec((1, W), lambda b, h, c: (0, h)),
                  pl.BlockSpec(wall.shape, lambda b, h, c: (0, 0)),
                  pl.BlockSpec(lvl.shape, lambda b, h, c: (0, 0))],
        out_specs=pl.BlockSpec((None, L, W), lambda b, h, c: (b, c, h)),
        out_shape=jax.ShapeDtypeStruct((B, S, D), BF16),
        scratch_shapes=[pltpu.VMEM((2, HEAD_DIM, HEAD_DIM), F32)],
        compiler_params=_params("parallel", "parallel", "arbitrary"),
        name="hgrn2_mixer",
    )(proj, proj, proj, proj, lb_logits.astype(F32), gnorm.reshape(1, MIX).astype(F32),
      jnp.asarray(wall, BF16), jnp.asarray(lvl))


def _sb_consts():
    r = np.arange(LANES)
    u = np.concatenate([(r[:, None] > r[None, :]), np.ones((LANES, LANES), bool)], axis=1)
    return np.concatenate([u, u], axis=0).astype(np.float32)


def _sb_kernel(q_ref, k_ref, v_ref, u_ref, o_ref, z_ref, w_ref, *, T, G):
    i = pl.program_id(2)
    uu = u_ref[...]

    def rows_of(j):
        return slice(j * T, (j + 1) * T) if isinstance(j, int) else pl.ds(pl.multiple_of(j * T, T), T)

    def lanes_of(g):
        return slice(g * HEAD_DIM, (g + 1) * HEAD_DIM)

    def scores(g, j):
        return _dot_nt(q_ref[:, lanes_of(g)], k_ref[rows_of(j), lanes_of(g)])

    def weighted_values(g, w, j):
        return _dot(w, v_ref[rows_of(j), lanes_of(g)])

    def weights(z, carry, diagonal):
        neg_abs = lax.bitcast_convert_type(lax.bitcast_convert_type(z, jnp.uint32) | jnp.uint32(0x80000000), F32)
        sp = jnp.maximum(z, 0.0) + jnp.log(1.0 + jnp.exp2(neg_abs)) * LOG2E
        log_beta = z - sp
        if diagonal:
            strictly_earlier = (lax.broadcasted_iota(jnp.int32, (T, T), 0)
                                > lax.broadcasted_iota(jnp.int32, (T, T), 1))
            sp = jnp.where(strictly_earlier, sp, 0.0)
        hi = sp.astype(BF16)
        lo = (sp - hi.astype(F32)).astype(BF16)
        ws = [None] * (T // LANES)
        for c in reversed(range(T // LANES)):
            seg = slice(c * LANES, (c + 1) * LANES)
            cs = _dot(jnp.concatenate([hi[:, seg], lo[:, seg]], axis=1), uu)
            w = jnp.exp2(log_beta[:, seg] - cs[:, :LANES] - carry)
            if diagonal:
                w = jnp.where(strictly_earlier[:, seg], w, 0.0)
            ws[c] = w.astype(BF16)
            carry = carry + cs[:, LANES:]
        return jnp.concatenate(ws, axis=1), carry

    def first(g):
        w_ref[g], carry = weights(scores(g, i), jnp.zeros((T, LANES), F32), True)
        z_ref[g] = scores(g, jnp.maximum(i - 1, 0))
        return carry, jnp.zeros((T, HEAD_DIM), F32)

    def step(g, j, carry, acc):
        acc = acc + weighted_values(g, w_ref[g], j + 1)
        w, carry = weights(z_ref[g], carry, False)
        z_ref[g] = scores(g, jnp.maximum(j - 1, 0))
        w_ref[g] = w
        return carry, acc

    state = tuple(first(g) for g in range(G))
    state = lax.fori_loop(0, i, lambda jj, st: tuple(step(g, i - 1 - jj, *st[g]) for g in range(G)), state)
    for g in range(G):
        o_ref[:, lanes_of(g)] = (state[g][1] + weighted_values(g, w_ref[g], 0)).astype(o_ref.dtype)


def _sb_mixer(proj, MIX, D):
    B, S, _ = proj.shape
    H = MIX // HEAD_DIM
    T = _tile(S, SB_TILE, LANES)
    G = SB_HEADS_PER_STEP
    assert H % G == 0
    HG = H // G
    W = G * HEAD_DIM
    u = _sb_consts()
    return pl.pallas_call(
        functools.partial(_sb_kernel, T=T, G=G),
        grid=(B, HG, S // T),
        in_specs=[pl.BlockSpec((None, T, W), lambda b, h, i: (b, i, h)),
                  pl.BlockSpec((None, S, W), lambda b, h, i: (b, 0, HG + h)),
                  pl.BlockSpec((None, S, W), lambda b, h, i: (b, 0, 2 * HG + h)),
                  pl.BlockSpec(u.shape, lambda b, h, i: (0, 0))],
        out_specs=pl.BlockSpec((None, T, W), lambda b, h, i: (b, i, h)),
        out_shape=jax.ShapeDtypeStruct((B, S, D), BF16),
        scratch_shapes=[pltpu.VMEM((G, T, T), F32), pltpu.VMEM((G, T, T), BF16)],
        compiler_params=_params("parallel", "parallel", "arbitrary"),
        name="stickbreaking_mixer",
    )(proj, proj, proj, jnp.asarray(u, BF16))


def _memattn_kernel(q_ref, mk_ref, mv_ref, mixed_ref, o_ref, *, scale):
    del mixed_ref
    s = _dot_nt(q_ref[...].astype(BF16), mk_ref[...]) * scale
    e = jnp.exp(s - jnp.max(s, axis=-1, keepdims=True))
    p = e / jnp.sum(e, axis=-1, keepdims=True)
    o_ref[...] = _dot(p.astype(BF16), mv_ref[...]).astype(o_ref.dtype)


def _memattn(proj, kv, mixed, MIX):
    B, S, P = proj.shape
    D = mixed.shape[-1]
    MEMW = D - MIX
    dm = MEMW // MEM_HEADS
    NM = kv.shape[1]
    ts = _tile(S, TILE_PREFS["mem_s"], SUBLANES)
    q_off = (P - MEMW) // dm
    o_off = MIX // dm
    return pl.pallas_call(
        functools.partial(_memattn_kernel, scale=dm ** -0.5),
        grid=(B, MEM_HEADS, S // ts),
        in_specs=[pl.BlockSpec((None, ts, dm), lambda b, h, t: (b, t, q_off + h)),
                  pl.BlockSpec((None, NM, dm), lambda b, h, t: (b, 0, h)),
                  pl.BlockSpec((None, NM, dm), lambda b, h, t: (b, 0, MEM_HEADS + h)),
                  pl.BlockSpec(memory_space=pl.ANY)],
        out_specs=pl.BlockSpec((None, ts, dm), lambda b, h, t: (b, t, o_off + h)),
        out_shape=jax.ShapeDtypeStruct(mixed.shape, mixed.dtype),
        input_output_aliases={3: 0},
        compiler_params=_params("parallel", "parallel", "parallel"),
        name="memory_attention",
    )(proj, kv, kv, mixed)


def _ffn_up_kernel(h_ref, wg_ref, wv_ref, cwg_ref, cwv_ref, cbg_ref, cbv_ref, o_ref, ug_ref, uv_ref, hg_ref, hv_ref,
                   *, tm, tiles_per_seq):
    m = pl.program_id(0)
    n = pl.program_id(1)
    h = h_ref[...]
    seq_start = (m % tiles_per_seq) == 0

    def conv_branch(w_ref, cw_ref, cb_ref, u_ref, halo_ref):
        u = _dot(h, w_ref[...].astype(BF16))
        u_ref[SUBLANES:SUBLANES + tm, :] = u
        u_ref[0:SUBLANES, :] = jnp.where(seq_start, 0.0, halo_ref[n])
        halo_ref[n] = u[tm - SUBLANES:tm, :]
        cw = cw_ref[...]
        return (cb_ref[...] + cw[2:3] * u
                + cw[1:2] * u_ref[SUBLANES - 1:SUBLANES - 1 + tm, :]
                + cw[0:1] * u_ref[SUBLANES - 2:SUBLANES - 2 + tm, :])

    gate = conv_branch(wg_ref, cwg_ref, cbg_ref, ug_ref, hg_ref)
    val = conv_branch(wv_ref, cwv_ref, cbv_ref, uv_ref, hv_ref)
    o_ref[...] = (gate * jax.nn.sigmoid(gate) * val).astype(o_ref.dtype)


def _ffn_up(h, w_up, conv_w, conv_b, layer, S):
    T, D = h.shape
    F = w_up.shape[-1] // 2
    tm = _tile(S, TILE_PREFS["up_m"], SUBLANES)
    tn = _tile(F, TILE_PREFS["up_n"], LANES)
    nt = F // tn
    assert conv_w.shape[1] == CONV_W and conv_b.shape[1] == 1
    lo = lambda m, n: (layer, 0, n)
    hi = lambda m, n: (layer, 0, nt + n)
    return pl.pallas_call(
        functools.partial(_ffn_up_kernel, tm=tm, tiles_per_seq=S // tm),
        grid=(T // tm, nt),
        in_specs=[pl.BlockSpec((tm, D), lambda m, n: (m, 0)),
                  pl.BlockSpec((None, D, tn), lo), pl.BlockSpec((None, D, tn), hi),
                  pl.BlockSpec((None, CONV_W, tn), lo), pl.BlockSpec((None, CONV_W, tn), hi),
                  pl.BlockSpec((None, 1, tn), lo), pl.BlockSpec((None, 1, tn), hi)],
        out_specs=pl.BlockSpec((tm, tn), lambda m, n: (m, n)),
        out_shape=jax.ShapeDtypeStruct((T, F), BF16),
        scratch_shapes=[pltpu.VMEM((tm + SUBLANES, tn), F32), pltpu.VMEM((tm + SUBLANES, tn), F32),
                        pltpu.VMEM((nt, SUBLANES, tn), F32), pltpu.VMEM((nt, SUBLANES, tn), F32)],
        compiler_params=_params("arbitrary", "arbitrary"),
        name="ffn_up_conv_gate",
    )(h, w_up, w_up, conv_w, conv_w, conv_b, conv_b)


def kernel(x, mem, mem_norm, mem_w_kv, lb_logits, mix_norm, hgrn_w_in, hgrn_norm, hgrn_w_out, sb_w_in, sb_w_out,
           ffn_norm, ffn_w_up, ffn_conv_w, ffn_conv_b, ffn_w_down, final_norm):
    B, S, D = x.shape
    T = B * S
    depth, MIX = lb_logits.shape
    NM = mem.shape[1]

    hm = _rmsnorm(mem.reshape(B * NM, D), mem_norm, BF16)
    kv = _matmul(hm, mem_w_kv, BF16, name="mem_kv_proj").reshape(B, NM, -1)

    sb_cols = sb_w_in.shape[-1]
    sb_scale = jnp.where(jnp.arange(sb_cols) < MIX, LOG2E / math.sqrt(HEAD_DIM), 1.0).astype(F32)
    w_down = ffn_w_down.astype(BF16)
    conv_b = ffn_conv_b.reshape(depth, 1, -1).astype(F32)
    conv_w = ffn_conv_w.astype(F32)

    x2 = x.reshape(T, D)
    for i in range(depth):
        j = i // 2
        h = _rmsnorm(x2, mix_norm[i], BF16)
        if i % 2 == 0:
            proj = _matmul(h, hgrn_w_in, F32, layer=j, name="hgrn_in_proj").reshape(B, S, -1)
            mixed = _hgrn_mixer(proj, lb_logits, hgrn_norm[j], i, D)
            w_out = hgrn_w_out
        else:
            proj = _matmul(h, sb_w_in, BF16, layer=j, col_scale=sb_scale, name="sb_in_proj").reshape(B, S, -1)
            mixed = _sb_mixer(proj, MIX, D)
            w_out = sb_w_out
        mixed = _memattn(proj, kv, mixed, MIX)
        x2 = _matmul(mixed.reshape(T, D), w_out, F32, layer=j, residual=x2, name="out_proj")

        h = _rmsnorm(x2, ffn_norm[i], BF16)
        act = _ffn_up(h, ffn_w_up, conv_w, conv_b, i, S)
        x2 = _matmul(act, w_down, F32, layer=i, residual=x2,
                     tm_pref=TILE_PREFS["down_m"], tn_pref=TILE_PREFS["down_n"], name="ffn_down_proj")

    return _rmsnorm(x2, final_norm, F32).reshape(B, S, D)
```

```python
import functools
import math

import numpy as np
import jax
import jax.numpy as jnp
from jax import lax
from jax.experimental import pallas as pl
from jax.experimental.pallas import tpu as pltpu

F32 = jnp.float32
BF16 = jnp.bfloat16

EPS = 1e-6
F_FLOOR = 1e-20
LOG2E = 1.4426950408889634
HEAD_DIM = 128
MEM_HEADS = 4
CONV_W = 3
LANES = 128
SUBLANES = 8
VMEM_LIMIT_BYTES = 56 * 1024 * 1024

HGRN_CHUNK = 128
SB_TILE = 512
SB_HEADS_PER_STEP = 3
TILE_PREFS = dict(norm_m=256, mm_m=1024, mm_n=512, up_m=1024, up_n=256, down_m=512, down_n=512,
                  hgrn_l=1024, mem_s=1024)


def _tile(dim, pref, quantum):
    t = min(pref, dim)
    t -= t % quantum
    while t > quantum and dim % t:
        t -= quantum
    assert t >= quantum and dim % t == 0, (dim, pref, quantum)
    return t


def _params(*sem):
    return pltpu.CompilerParams(dimension_semantics=sem, vmem_limit_bytes=VMEM_LIMIT_BYTES)


def _dot(a, b):
    return jnp.dot(a, b, preferred_element_type=F32)


def _dot_nt(a, b):
    return lax.dot_general(a, b, (((1,), (1,)), ((), ())), preferred_element_type=F32)


def _dot_tn(a, b):
    return lax.dot_general(a, b, (((0,), (0,)), ((), ())), preferred_element_type=F32)


def _rmsnorm_kernel(x_ref, g_ref, o_ref):
    x = x_ref[...]
    ms = jnp.mean(x * x, axis=-1, keepdims=True)
    o_ref[...] = (x * lax.rsqrt(ms + EPS) * g_ref[...]).astype(o_ref.dtype)


def _rmsnorm(x2d, g, out_dtype):
    T, D = x2d.shape
    tm = _tile(T, TILE_PREFS["norm_m"], SUBLANES)
    return pl.pallas_call(
        _rmsnorm_kernel,
        grid=(T // tm,),
        in_specs=[pl.BlockSpec((tm, D), lambda i: (i, 0)), pl.BlockSpec((1, D), lambda i: (0, 0))],
        out_specs=pl.BlockSpec((tm, D), lambda i: (i, 0)),
        out_shape=jax.ShapeDtypeStruct((T, D), out_dtype),
        compiler_params=_params("parallel"),
        name="rmsnorm",
    )(x2d, g.reshape(1, D).astype(F32))


def _mm_kernel(a_ref, w_ref, o_ref):
    o_ref[...] = _dot(a_ref[...], w_ref[...].astype(BF16)).astype(o_ref.dtype)


def _mm_scale_kernel(a_ref, w_ref, s_ref, o_ref):
    o_ref[...] = (_dot(a_ref[...], w_ref[...].astype(BF16)) * s_ref[...]).astype(o_ref.dtype)


def _mm_res_kernel(a_ref, w_ref, r_ref, o_ref):
    o_ref[...] = r_ref[...] + _dot(a_ref[...], w_ref[...].astype(BF16))


def _matmul(a, w, out_dtype, layer=None, residual=None, col_scale=None, tm_pref=None, tn_pref=None, name="matmul"):
    M, K = a.shape
    N = w.shape[-1]
    tm = _tile(M, tm_pref or TILE_PREFS["mm_m"], SUBLANES)
    tn = _tile(N, tn_pref or TILE_PREFS["mm_n"], LANES)
    if layer is None:
        w_spec = pl.BlockSpec((K, tn), lambda m, n: (0, n))
    else:
        w_spec = pl.BlockSpec((None, K, tn), lambda m, n: (layer, 0, n))
    in_specs = [pl.BlockSpec((tm, K), lambda m, n: (m, 0)), w_spec]
    args = [a, w]
    kern = _mm_kernel
    assert residual is None or col_scale is None
    if residual is not None:
        in_specs.append(pl.BlockSpec((tm, tn), lambda m, n: (m, n)))
        args.append(residual)
        kern = _mm_res_kernel
    if col_scale is not None:
        in_specs.append(pl.BlockSpec((1, tn), lambda m, n: (0, n)))
        args.append(col_scale.reshape(1, N).astype(F32))
        kern = _mm_scale_kernel
    return pl.pallas_call(
        kern,
        grid=(M // tm, N // tn),
        in_specs=in_specs,
        out_specs=pl.BlockSpec((tm, tn), lambda m, n: (m, n)),
        out_shape=jax.ShapeDtypeStruct((M, N), out_dtype),
        compiler_params=_params("parallel", "arbitrary"),
        name=name,
    )(*args)


def _hgrn_consts(C):
    nlev = int(math.log2(C))
    assert 1 << nlev == C
    r = np.arange(C)
    rr, jj = r[:, None], r[None, :]
    blocks = []
    for l in range(nlev):
        c = 1 << l
        ref = ((r & ~(2 * c - 1)) + c - 1)[:, None]
        odd = ((r & c) != 0)[:, None]
        blocks.append(np.where(odd, (jj > ref) & (jj <= rr), (jj > rr) & (jj <= ref)))
    blocks.append(jj <= rr)
    blocks.append(jj > rr)
    wall = np.concatenate(blocks, axis=0).astype(np.float32)
    x = rr ^ jj
    lvl = np.where(x == 0, -1, np.floor(np.log2(np.maximum(x, 1))).astype(np.int64))
    lvl = np.where(jj > rr, -2, lvl).astype(np.int32)
    return wall, lvl, nlev


def _hgrn_kernel(q_ref, zf_ref, v_ref, zg_ref, lbl_ref, gn_ref, wall_ref, lvl_ref, o_ref, st_ref,
                 *, layer, C, nlev, n_chunks):
    @pl.when(pl.program_id(2) == 0)
    def _():
        st_ref[...] = jnp.zeros_like(st_ref)

    lbl = lbl_ref[...]
    e = jnp.exp(lbl - jnp.max(lbl, axis=0, keepdims=True))
    p = e / jnp.sum(e, axis=0, keepdims=True)
    cs = p[0:1]
    for j in range(1, layer + 1):
        cs = cs + p[j:j + 1]
    lb = jnp.maximum(cs - p[0:1], 0.0)
    gn = gn_ref[...]
    lvl = lvl_ref[...]

    sts = [st_ref[0], st_ref[1]]
    for c in range(n_chunks):
        rows = slice(c * C, (c + 1) * C)
        zf = zf_ref[rows, :]
        f = lb + (1.0 - lb) * jax.nn.sigmoid(zf)
        g = jnp.log(jnp.maximum(f, F_FLOOR))
        kin2 = (1.0 - lb) * jax.nn.sigmoid(-zf)
        g_hi = g.astype(BF16)
        g_lo = (g - g_hi.astype(F32)).astype(BF16)
        d2 = _dot(wall_ref[...], jnp.concatenate([g_hi, g_lo], axis=0))
        og2 = []
        for hd in range(2):
            head = slice(hd * HEAD_DIM, (hd + 1) * HEAD_DIM)
            d = d2[:, head]
            q = q_ref[rows, head]
            v = v_ref[rows, head].astype(BF16)
            kin = kin2[:, head]
            scores = jnp.where(lvl == -1, _dot_nt(q.astype(BF16), kin.astype(BF16)), 0.0)
            for l in range(nlev):
                el = jnp.exp(d[l * C:(l + 1) * C])
                pl_ = _dot_nt((q * el).astype(BF16), (kin * el).astype(BF16))
                scores = jnp.where(lvl == l, pl_, scores)
            b = d[nlev * C:(nlev + 1) * C]
            b_rest = d[(nlev + 1) * C:(nlev + 2) * C]

            st = sts[hd]
            inter = _dot_nt((q * jnp.exp(b)).astype(BF16), st.astype(BF16))
            intra = _dot(scores.astype(BF16), v)
            sts[hd] = st * jnp.exp(b[C - 1:C, :]) + _dot_tn(v, (kin * jnp.exp(b_rest)).astype(BF16))

            og = (inter + intra) * jax.nn.sigmoid(zg_ref[rows, head])
            ms = jnp.mean(og * og, axis=-1, keepdims=True)
            og2.append(og * lax.rsqrt(ms + EPS))
        o_ref[rows, :] = (jnp.concatenate(og2, axis=1) * gn).astype(o_ref.dtype)
    st_ref[0] = sts[0]
    st_ref[1] = sts[1]


def _hgrn_mixer(proj, lb_logits, gnorm, layer, D):
    B, S, _ = proj.shape
    depth, MIX = lb_logits.shape
    W = 2 * HEAD_DIM
    HP = MIX // W
    assert MIX % W == 0
    C = min(HGRN_CHUNK, S)
    L = _tile(S, max(TILE_PREFS["hgrn_l"], C), C)
    wall, lvl, nlev = _hgrn_consts(C)
    wall = np.concatenate([wall, wall], axis=1)
    blk = lambda off: pl.BlockSpec((None, L, W), lambda b, h, c: (b, c, off + h))
    return pl.pallas_call(
        functools.partial(_hgrn_kernel, layer=layer, C=C, nlev=nlev, n_chunks=L // C),
        grid=(B, HP, S // L),
        in_specs=[blk(0), blk(HP), blk(2 * HP), blk(3 * HP),
                  pl.BlockSpec((depth, W), lambda b, h, c: (0, h)),
                  pl.BlockSpec((1, W), lambda b, h, c: (0, h)),
                  pl.BlockSpec(wall.shape, lambda b, h, c: (0, 0)),
                  pl.BlockSpec(lvl.shape, lambda b, h, c: (0, 0))],
        out_specs=pl.BlockSpec((None, L, W), lambda b, h, c: (b, c, h)),
        out_shape=jax.ShapeDtypeStruct((B, S, D), BF16),
        scratch_shapes=[pltpu.VMEM((2, HEAD_DIM, HEAD_DIM), F32)],
        compiler_params=_params("parallel", "parallel", "arbitrary"),
        name="hgrn2_mixer",
    )(proj, proj, proj, proj, lb_logits.astype(F32), gnorm.reshape(1, MIX).astype(F32),
      jnp.asarray(wall, BF16), jnp.asarray(lvl))


def _sb_consts():
    r = np.arange(LANES)
    u = np.concatenate([(r[:, None] > r[None, :]), np.ones((LANES, LANES), bool)], axis=1)
    return np.concatenate([u, u], axis=0).astype(np.float32)


def _sb_kernel(q_ref, k_ref, v_ref, u_ref, o_ref, z_ref, w_ref, *, T, G):
    i = pl.program_id(2)
    uu = u_ref[...]

    def rows_of(j):
        return slice(j * T, (j + 1) * T) if isinstance(j, int) else pl.ds(pl.multiple_of(j * T, T), T)

    def lanes_of(g):
        return slice(g * HEAD_DIM, (g + 1) * HEAD_DIM)

    def scores(g, j):
        return _dot_nt(q_ref[:, lanes_of(g)], k_ref[rows_of(j), lanes_of(g)])

    def weighted_values(g, w, j):
        return _dot(w, v_ref[rows_of(j), lanes_of(g)])

    def weights(z, carry, diagonal):
        neg_abs = lax.bitcast_convert_type(lax.bitcast_convert_type(z, jnp.uint32) | jnp.uint32(0x80000000), F32)
        sp = jnp.maximum(z, 0.0) + jnp.log(1.0 + jnp.exp2(neg_abs)) * LOG2E
        log_beta = z - sp
        if diagonal:
            strictly_earlier = (lax.broadcasted_iota(jnp.int32, (T, T), 0)
                                > lax.broadcasted_iota(jnp.int32, (T, T), 1))
            sp = jnp.where(strictly_earlier, sp, 0.0)
        hi = sp.astype(BF16)
        lo = (sp - hi.astype(F32)).astype(BF16)
        ws = [None] * (T // LANES)
        for c in reversed(range(T // LANES)):
            seg = slice(c * LANES, (c + 1) * LANES)
            cs = _dot(jnp.concatenate([hi[:, seg], lo[:, seg]], axis=1), uu)
            w = jnp.exp2(log_beta[:, seg] - cs[:, :LANES] - carry)
            if diagonal:
                w = jnp.where(strictly_earlier[:, seg], w, 0.0)
            ws[c] = w.astype(BF16)
            carry = carry + cs[:, LANES:]
        return jnp.concatenate(ws, axis=1), carry

    def first(g):
        w_ref[g], carry = weights(scores(g, i), jnp.zeros((T, LANES), F32), True)
        z_ref[0, g] = scores(g, jnp.maximum(i - 1, 0))
        return carry, jnp.zeros((T, HEAD_DIM), F32)

    def step(g, j, slot, carry, acc):
        acc = acc + weighted_values(g, w_ref[g], j + 1)
        w, carry = weights(z_ref[slot, g], carry, False)
        z_ref[1 - slot, g] = scores(g, jnp.maximum(j - 1, 0))
        w_ref[g] = w
        return carry, acc

    def two_tiles(p, state):
        j = i - 1 - 2 * p
        state = tuple(step(g, j, 0, *state[g]) for g in range(G))
        return tuple(step(g, j - 1, 1, *state[g]) for g in range(G))

    def last_tile(_, state):
        return tuple(step(g, 0, 0, *state[g]) for g in range(G))

    state = tuple(first(g) for g in range(G))
    state = lax.fori_loop(0, i // 2, two_tiles, state)
    state = lax.fori_loop(0, i % 2, last_tile, state)
    for g in range(G):
        o_ref[:, lanes_of(g)] = (state[g][1] + weighted_values(g, w_ref[g], 0)).astype(o_ref.dtype)


def _sb_mixer(proj, MIX, D):
    B, S, _ = proj.shape
    H = MIX // HEAD_DIM
    T = _tile(S, SB_TILE, LANES)
    G = SB_HEADS_PER_STEP
    assert H % G == 0
    HG = H // G
    W = G * HEAD_DIM
    u = _sb_consts()
    return pl.pallas_call(
        functools.partial(_sb_kernel, T=T, G=G),
        grid=(B, HG, S // T),
        in_specs=[pl.BlockSpec((None, T, W), lambda b, h, i: (b, i, h)),
                  pl.BlockSpec((None, S, W), lambda b, h, i: (b, 0, HG + h)),
                  pl.BlockSpec((None, S, W), lambda b, h, i: (b, 0, 2 * HG + h)),
                  pl.BlockSpec(u.shape, lambda b, h, i: (0, 0))],
        out_specs=pl.BlockSpec((None, T, W), lambda b, h, i: (b, i, h)),
        out_shape=jax.ShapeDtypeStruct((B, S, D), BF16),
        scratch_shapes=[pltpu.VMEM((2, G, T, T), F32), pltpu.VMEM((G, T, T), BF16)],
        compiler_params=_params("parallel", "parallel", "arbitrary"),
        name="stickbreaking_mixer",
    )(proj, proj, proj, jnp.asarray(u, BF16))


def _memattn_kernel(q_ref, mk_ref, mv_ref, mixed_ref, o_ref, *, scale):
    del mixed_ref
    s = _dot_nt(q_ref[...].astype(BF16), mk_ref[...]) * scale
    e = jnp.exp(s - jnp.max(s, axis=-1, keepdims=True))
    p = e / jnp.sum(e, axis=-1, keepdims=True)
    o_ref[...] = _dot(p.astype(BF16), mv_ref[...]).astype(o_ref.dtype)


def _memattn(proj, kv, mixed, MIX):
    B, S, P = proj.shape
    D = mixed.shape[-1]
    MEMW = D - MIX
    dm = MEMW // MEM_HEADS
    NM = kv.shape[1]
    ts = _tile(S, TILE_PREFS["mem_s"], SUBLANES)
    q_off = (P - MEMW) // dm
    o_off = MIX // dm
    return pl.pallas_call(
        functools.partial(_memattn_kernel, scale=dm ** -0.5),
        grid=(B, MEM_HEADS, S // ts),
        in_specs=[pl.BlockSpec((None, ts, dm), lambda b, h, t: (b, t, q_off + h)),
                  pl.BlockSpec((None, NM, dm), lambda b, h, t: (b, 0, h)),
                  pl.BlockSpec((None, NM, dm), lambda b, h, t: (b, 0, MEM_HEADS + h)),
                  pl.BlockSpec(memory_space=pl.ANY)],
        out_specs=pl.BlockSpec((None, ts, dm), lambda b, h, t: (b, t, o_off + h)),
        out_shape=jax.ShapeDtypeStruct(mixed.shape, mixed.dtype),
        input_output_aliases={3: 0},
        compiler_params=_params("parallel", "parallel", "parallel"),
        name="memory_attention",
    )(proj, kv, kv, mixed)


def _ffn_up_kernel(h_ref, wg_ref, wv_ref, cwg_ref, cwv_ref, cbg_ref, cbv_ref, o_ref, ug_ref, uv_ref, hg_ref, hv_ref,
                   *, tm, tiles_per_seq):
    m = pl.program_id(0)
    n = pl.program_id(1)
    h = h_ref[...]
    seq_start = (m % tiles_per_seq) == 0

    def conv_branch(w_ref, cw_ref, cb_ref, u_ref, halo_ref):
        u = _dot(h, w_ref[...].astype(BF16))
        u_ref[SUBLANES:SUBLANES + tm, :] = u
        u_ref[0:SUBLANES, :] = jnp.where(seq_start, 0.0, halo_ref[n])
        halo_ref[n] = u[tm - SUBLANES:tm, :]
        cw = cw_ref[...]
        return (cb_ref[...] + cw[2:3] * u
                + cw[1:2] * u_ref[SUBLANES - 1:SUBLANES - 1 + tm, :]
                + cw[0:1] * u_ref[SUBLANES - 2:SUBLANES - 2 + tm, :])

    gate = conv_branch(wg_ref, cwg_ref, cbg_ref, ug_ref, hg_ref)
    val = conv_branch(wv_ref, cwv_ref, cbv_ref, uv_ref, hv_ref)
    o_ref[...] = (gate * jax.nn.sigmoid(gate) * val).astype(o_ref.dtype)


def _ffn_up(h, w_up, conv_w, conv_b, layer, S):
    T, D = h.shape
    F = w_up.shape[-1] // 2
    tm = _tile(S, TILE_PREFS["up_m"], SUBLANES)
    tn = _tile(F, TILE_PREFS["up_n"], LANES)
    nt = F // tn
    assert conv_w.shape[1] == CONV_W and conv_b.shape[1] == 1
    lo = lambda m, n: (layer, 0, n)
    hi = lambda m, n: (layer, 0, nt + n)
    return pl.pallas_call(
        functools.partial(_ffn_up_kernel, tm=tm, tiles_per_seq=S // tm),
        grid=(T // tm, nt),
        in_specs=[pl.BlockSpec((tm, D), lambda m, n: (m, 0)),
                  pl.BlockSpec((None, D, tn), lo), pl.BlockSpec((None, D, tn), hi),
                  pl.BlockSpec((None, CONV_W, tn), lo), pl.BlockSpec((None, CONV_W, tn), hi),
                  pl.BlockSpec((None, 1, tn), lo), pl.BlockSpec((None, 1, tn), hi)],
        out_specs=pl.BlockSpec((tm, tn), lambda m, n: (m, n)),
        out_shape=jax.ShapeDtypeStruct((T, F), BF16),
        scratch_shapes=[pltpu.VMEM((tm + SUBLANES, tn), F32), pltpu.VMEM((tm + SUBLANES, tn), F32),
                        pltpu.VMEM((nt, SUBLANES, tn), F32), pltpu.VMEM((nt, SUBLANES, tn), F32)],
        compiler_params=_params("arbitrary", "arbitrary"),
        name="ffn_up_conv_gate",
    )(h, w_up, w_up, conv_w, conv_w, conv_b, conv_b)


def kernel(x, mem, mem_norm, mem_w_kv, lb_logits, mix_norm, hgrn_w_in, hgrn_norm, hgrn_w_out, sb_w_in, sb_w_out,
           ffn_norm, ffn_w_up, ffn_conv_w, ffn_conv_b, ffn_w_down, final_norm):
    B, S, D = x.shape
    T = B * S
    depth, MIX = lb_logits.shape
    NM = mem.shape[1]

    hm = _rmsnorm(mem.reshape(B * NM, D), mem_norm, BF16)
    kv = _matmul(hm, mem_w_kv, BF16, name="mem_kv_proj").reshape(B, NM, -1)

    sb_cols = sb_w_in.shape[-1]
    sb_scale = jnp.where(jnp.arange(sb_cols) < MIX, LOG2E / math.sqrt(HEAD_DIM), 1.0).astype(F32)
    w_down = ffn_w_down.astype(BF16)
    conv_b = ffn_conv_b.reshape(depth, 1, -1).astype(F32)
    conv_w = ffn_conv_w.astype(F32)

    x2 = x.reshape(T, D)
    for i in range(depth):
        j = i // 2
        h = _rmsnorm(x2, mix_norm[i], BF16)
        if i % 2 == 0:
            proj = _matmul(h, hgrn_w_in, F32, layer=j, name="hgrn_in_proj").reshape(B, S, -1)
            mixed = _hgrn_mixer(proj, lb_logits, hgrn_norm[j], i, D)
            w_out = hgrn_w_out
        else:
            proj = _matmul(h, sb_w_in, BF16, layer=j, col_scale=sb_scale, name="sb_in_proj").reshape(B, S, -1)
            mixed = _sb_mixer(proj, MIX, D)
            w_out = sb_w_out
        mixed = _memattn(proj, kv, mixed, MIX)
        x2 = _matmul(mixed.reshape(T, D), w_out, F32, layer=j, residual=x2, name="out_proj")

        h = _rmsnorm(x2, ffn_norm[i], BF16)
        act = _ffn_up(h, ffn_w_up, conv_w, conv_b, i, S)
        x2 = _matmul(act, w_down, F32, layer=i, residual=x2,
                     tm_pref=TILE_PREFS["down_m"], tn_pref=TILE_PREFS["down_n"], name="ffn_down_proj")

    return _rmsnorm(x2, final_norm, F32).reshape(B, S, D)
```

```python
import functools
import math

import numpy as np
import jax
import jax.numpy as jnp
from jax import lax
from jax.experimental import pallas as pl
from jax.experimental.pallas import tpu as pltpu

F32 = jnp.float32
BF16 = jnp.bfloat16

EPS = 1e-6
F_FLOOR = 1e-20
LOG2E = 1.4426950408889634
HEAD_DIM = 128
MEM_HEADS = 4
CONV_W = 3
LANES = 128
SUBLANES = 8
MXU_K = 256
VMEM_LIMIT_BYTES = 56 * 1024 * 1024

HGRN_CHUNK = 128
SB_TILE = 512
SB_HEADS_PER_STEP = 3
TILE_PREFS = dict(norm_m=256, mm_m=1024, mm_n=512, up_m=1024, up_n=256, down_m=512, down_n=512,
                  hgrn_l=1024, mem_s=1024)


def _tile(dim, pref, quantum):
    t = min(pref, dim)
    t -= t % quantum
    while t > quantum and dim % t:
        t -= quantum
    assert t >= quantum and dim % t == 0, (dim, pref, quantum)
    return t


def _params(*sem):
    return pltpu.CompilerParams(dimension_semantics=sem, vmem_limit_bytes=VMEM_LIMIT_BYTES)


def _dot(a, b):
    return jnp.dot(a, b, preferred_element_type=F32)


def _dot_nt(a, b):
    return lax.dot_general(a, b, (((1,), (1,)), ((), ())), preferred_element_type=F32)


def _dot_tn(a, b):
    return lax.dot_general(a, b, (((0,), (0,)), ((), ())), preferred_element_type=F32)


def _rmsnorm_kernel(x_ref, g_ref, o_ref):
    x = x_ref[...]
    ms = jnp.mean(x * x, axis=-1, keepdims=True)
    o_ref[...] = (x * lax.rsqrt(ms + EPS) * g_ref[...]).astype(o_ref.dtype)


def _rmsnorm(x2d, g, out_dtype):
    T, D = x2d.shape
    tm = _tile(T, TILE_PREFS["norm_m"], SUBLANES)
    return pl.pallas_call(
        _rmsnorm_kernel,
        grid=(T // tm,),
        in_specs=[pl.BlockSpec((tm, D), lambda i: (i, 0)), pl.BlockSpec((1, D), lambda i: (0, 0))],
        out_specs=pl.BlockSpec((tm, D), lambda i: (i, 0)),
        out_shape=jax.ShapeDtypeStruct((T, D), out_dtype),
        compiler_params=_params("parallel"),
        name="rmsnorm",
    )(x2d, g.reshape(1, D).astype(F32))


def _mm_kernel(a_ref, w_ref, o_ref):
    o_ref[...] = _dot(a_ref[...], w_ref[...].astype(BF16)).astype(o_ref.dtype)


def _mm_scale_kernel(a_ref, w_ref, s_ref, o_ref):
    o_ref[...] = (_dot(a_ref[...], w_ref[...].astype(BF16)) * s_ref[...]).astype(o_ref.dtype)


def _mm_res_kernel(a_ref, w_ref, r_ref, o_ref):
    o_ref[...] = r_ref[...] + _dot(a_ref[...], w_ref[...].astype(BF16))


def _matmul(a, w, out_dtype, layer=None, residual=None, col_scale=None, tm_pref=None, tn_pref=None, name="matmul"):
    M, K = a.shape
    N = w.shape[-1]
    tm = _tile(M, tm_pref or TILE_PREFS["mm_m"], SUBLANES)
    tn = _tile(N, tn_pref or TILE_PREFS["mm_n"], LANES)
    if layer is None:
        w_spec = pl.BlockSpec((K, tn), lambda m, n: (0, n))
    else:
        w_spec = pl.BlockSpec((None, K, tn), lambda m, n: (layer, 0, n))
    in_specs = [pl.BlockSpec((tm, K), lambda m, n: (m, 0)), w_spec]
    args = [a, w]
    kern = _mm_kernel
    assert residual is None or col_scale is None
    if residual is not None:
        in_specs.append(pl.BlockSpec((tm, tn), lambda m, n: (m, n)))
        args.append(residual)
        kern = _mm_res_kernel
    if col_scale is not None:
        in_specs.append(pl.BlockSpec((1, tn), lambda m, n: (0, n)))
        args.append(col_scale.reshape(1, N).astype(F32))
        kern = _mm_scale_kernel
    return pl.pallas_call(
        kern,
        grid=(M // tm, N // tn),
        in_specs=in_specs,
        out_specs=pl.BlockSpec((tm, tn), lambda m, n: (m, n)),
        out_shape=jax.ShapeDtypeStruct((M, N), out_dtype),
        compiler_params=_params("parallel", "arbitrary"),
        name=name,
    )(*args)


def _hgrn_consts(C):
    nlev = int(math.log2(C))
    assert 1 << nlev == C
    r = np.arange(C)
    rr, jj = r[:, None], r[None, :]
    blocks = []
    for l in range(nlev):
        c = 1 << l
        ref = ((r & ~(2 * c - 1)) + c - 1)[:, None]
        odd = ((r & c) != 0)[:, None]
        blocks.append(np.where(odd, (jj > ref) & (jj <= rr), (jj > rr) & (jj <= ref)))
    blocks.append(jj <= rr)
    blocks.append(jj > rr)
    wall = np.concatenate(blocks, axis=0).astype(np.float32)
    x = rr ^ jj
    lvl = np.where(x == 0, -1, np.floor(np.log2(np.maximum(x, 1))).astype(np.int64))
    lvl = np.where(jj > rr, -2, lvl).astype(np.int32)
    return wall, lvl, nlev


def _hgrn_kernel(q_ref, zf_ref, v_ref, zg_ref, lbl_ref, gn_ref, wall_ref, lvl_ref, o_ref, st_ref,
                 *, layer, C, nlev, n_chunks):
    @pl.when(pl.program_id(2) == 0)
    def _():
        st_ref[...] = jnp.zeros_like(st_ref)

    lbl = lbl_ref[...]
    e = jnp.exp(lbl - jnp.max(lbl, axis=0, keepdims=True))
    p = e / jnp.sum(e, axis=0, keepdims=True)
    cs = p[0:1]
    for j in range(1, layer + 1):
        cs = cs + p[j:j + 1]
    lb = jnp.maximum(cs - p[0:1], 0.0)
    gn = gn_ref[...]
    lvl = lvl_ref[...]

    sts = [st_ref[0], st_ref[1]]
    for c in range(n_chunks):
        rows = slice(c * C, (c + 1) * C)
        zf = zf_ref[rows, :]
        f = lb + (1.0 - lb) * jax.nn.sigmoid(zf)
        g = jnp.log(jnp.maximum(f, F_FLOOR))
        kin2 = (1.0 - lb) * jax.nn.sigmoid(-zf)
        g_hi = g.astype(BF16)
        g_lo = (g - g_hi.astype(F32)).astype(BF16)
        d2 = _dot(wall_ref[...], jnp.concatenate([g_hi, g_lo], axis=0))
        og2 = []
        for hd in range(2):
            head = slice(hd * HEAD_DIM, (hd + 1) * HEAD_DIM)
            d = d2[:, head]
            q = q_ref[rows, head]
            v = v_ref[rows, head].astype(BF16)
            kin = kin2[:, head]
            scores = jnp.where(lvl == -1, _dot_nt(q.astype(BF16), kin.astype(BF16)), 0.0)
            for l in range(nlev):
                el = jnp.exp(d[l * C:(l + 1) * C])
                pl_ = _dot_nt((q * el).astype(BF16), (kin * el).astype(BF16))
                scores = jnp.where(lvl == l, pl_, scores)
            b = d[nlev * C:(nlev + 1) * C]
            b_rest = d[(nlev + 1) * C:(nlev + 2) * C]

            st = sts[hd]
            inter = _dot_nt((q * jnp.exp(b)).astype(BF16), st.astype(BF16))
            intra = _dot(scores.astype(BF16), v)
            sts[hd] = st * jnp.exp(b[C - 1:C, :]) + _dot_tn(v, (kin * jnp.exp(b_rest)).astype(BF16))

            og = (inter + intra) * jax.nn.sigmoid(zg_ref[rows, head])
            ms = jnp.mean(og * og, axis=-1, keepdims=True)
            og2.append(og * lax.rsqrt(ms + EPS))
        o_ref[rows, :] = (jnp.concatenate(og2, axis=1) * gn).astype(o_ref.dtype)
    st_ref[0] = sts[0]
    st_ref[1] = sts[1]


def _hgrn_mixer(proj, lb_logits, gnorm, layer, D):
    B, S, _ = proj.shape
    depth, MIX = lb_logits.shape
    W = 2 * HEAD_DIM
    HP = MIX // W
    assert MIX % W == 0
    C = min(HGRN_CHUNK, S)
    L = _tile(S, max(TILE_PREFS["hgrn_l"], C), C)
    wall, lvl, nlev = _hgrn_consts(C)
    wall = np.concatenate([wall, wall], axis=1)
    blk = lambda off: pl.BlockSpec((None, L, W), lambda b, h, c: (b, c, off + h))
    return pl.pallas_call(
        functools.partial(_hgrn_kernel, layer=layer, C=C, nlev=nlev, n_chunks=L // C),
        grid=(B, HP, S // L),
        in_specs=[blk(0), blk(HP), blk(2 * HP), blk(3 * HP),
                  pl.BlockSpec((depth, W), lambda b, h, c: (0, h)),
                  pl.BlockSpec((1, W), lambda b, h, c: (0, h)),
                  pl.BlockSpec(wall.shape, lambda b, h, c: (0, 0)),
                  pl.BlockSpec(lvl.shape, lambda b, h, c: (0, 0))],
        out_specs=pl.BlockSpec((None, L, W), lambda b, h, c: (b, c, h)),
        out_shape=jax.ShapeDtypeStruct((B, S, D), BF16),
        scratch_shapes=[pltpu.VMEM((2, HEAD_DIM, HEAD_DIM), F32)],
        compiler_params=_params("parallel", "parallel", "arbitrary"),
        name="hgrn2_mixer",
    )(proj, proj, proj, proj, lb_logits.astype(F32), gnorm.reshape(1, MIX).astype(F32),
      jnp.asarray(wall, BF16), jnp.asarray(lvl))


def _sb_consts():
    r = np.arange(LANES)
    u = np.concatenate([(r[:, None] > r[None, :]), np.ones((LANES, LANES), bool)], axis=1)
    return np.concatenate([u, u], axis=0).astype(np.float32)


def _sb_kernel(q_ref, k_ref, v_ref, u_ref, o_ref, z_ref, w_ref, *, T, G):
    i = pl.program_id(2)
    uu = u_ref[...]

    def rows_of(j):
        return slice(j * T, (j + 1) * T) if isinstance(j, int) else pl.ds(pl.multiple_of(j * T, T), T)

    def lanes_of(g):
        return slice(g * HEAD_DIM, (g + 1) * HEAD_DIM)

    def scores(g, j):
        return _dot_nt(q_ref[:, lanes_of(g)], k_ref[rows_of(j), lanes_of(g)])

    def weighted_values(g, w, j):
        return _dot(w, v_ref[rows_of(j), lanes_of(g)])

    def weights(z, carry, diagonal):
        neg_abs = lax.bitcast_convert_type(lax.bitcast_convert_type(z, jnp.uint32) | jnp.uint32(0x80000000), F32)
        sp = jnp.maximum(z, 0.0) + jnp.log(1.0 + jnp.exp2(neg_abs)) * LOG2E
        log_beta = z - sp
        if diagonal:
            strictly_earlier = (lax.broadcasted_iota(jnp.int32, (T, T), 0)
                                > lax.broadcasted_iota(jnp.int32, (T, T), 1))
            sp = jnp.where(strictly_earlier, sp, 0.0)
        hi = sp.astype(BF16)
        lo = (sp - hi.astype(F32)).astype(BF16)
        ws = [None] * (T // LANES)
        for c in reversed(range(T // LANES)):
            seg = slice(c * LANES, (c + 1) * LANES)
            cs = _dot(jnp.concatenate([hi[:, seg], lo[:, seg]], axis=1), uu)
            w = jnp.exp2(log_beta[:, seg] - cs[:, :LANES] - carry)
            if diagonal:
                w = jnp.where(strictly_earlier[:, seg], w, 0.0)
            ws[c] = w.astype(BF16)
            carry = carry + cs[:, LANES:]
        return jnp.concatenate(ws, axis=1), carry

    def first(g):
        w_ref[g], carry = weights(scores(g, i), jnp.zeros((T, LANES), F32), True)
        z_ref[0, g] = scores(g, jnp.maximum(i - 1, 0))
        return carry, jnp.zeros((T, HEAD_DIM), F32)

    def step(g, j, slot, carry, acc):
        acc = acc + weighted_values(g, w_ref[g], j + 1)
        w, carry = weights(z_ref[slot, g], carry, False)
        z_ref[1 - slot, g] = scores(g, jnp.maximum(j - 1, 0))
        w_ref[g] = w
        return carry, acc

    def two_tiles(p, state):
        j = i - 1 - 2 * p
        state = tuple(step(g, j, 0, *state[g]) for g in range(G))
        return tuple(step(g, j - 1, 1, *state[g]) for g in range(G))

    def last_tile(_, state):
        return tuple(step(g, 0, 0, *state[g]) for g in range(G))

    state = tuple(first(g) for g in range(G))
    state = lax.fori_loop(0, i // 2, two_tiles, state)
    state = lax.fori_loop(0, i % 2, last_tile, state)
    for g in range(G):
        o_ref[:, lanes_of(g)] = (state[g][1] + weighted_values(g, w_ref[g], 0)).astype(o_ref.dtype)


def _sb_mixer(proj, MIX, D):
    B, S, _ = proj.shape
    H = MIX // HEAD_DIM
    T = _tile(S, SB_TILE, LANES)
    G = SB_HEADS_PER_STEP
    assert H % G == 0
    HG = H // G
    W = G * HEAD_DIM
    u = _sb_consts()
    return pl.pallas_call(
        functools.partial(_sb_kernel, T=T, G=G),
        grid=(B, HG, S // T),
        in_specs=[pl.BlockSpec((None, T, W), lambda b, h, i: (b, i, h)),
                  pl.BlockSpec((None, S, W), lambda b, h, i: (b, 0, HG + h)),
                  pl.BlockSpec((None, S, W), lambda b, h, i: (b, 0, 2 * HG + h)),
                  pl.BlockSpec(u.shape, lambda b, h, i: (0, 0))],
        out_specs=pl.BlockSpec((None, T, W), lambda b, h, i: (b, i, h)),
        out_shape=jax.ShapeDtypeStruct((B, S, D), BF16),
        scratch_shapes=[pltpu.VMEM((2, G, T, T), F32), pltpu.VMEM((G, T, T), BF16)],
        compiler_params=_params("parallel", "parallel", "arbitrary"),
        name="stickbreaking_mixer",
    )(proj, proj, proj, jnp.asarray(u, BF16))


def _memattn_kernel(q_ref, mk_ref, mv_ref, mixed_ref, o_ref, *, scale):
    del mixed_ref
    s = _dot_nt(q_ref[...].astype(BF16), mk_ref[...]) * scale
    e = jnp.exp(s - jnp.max(s, axis=-1, keepdims=True))
    p = e / jnp.sum(e, axis=-1, keepdims=True)
    o_ref[...] = _dot(p.astype(BF16), mv_ref[...]).astype(o_ref.dtype)


def _memattn(proj, kv, mixed, MIX):
    B, S, P = proj.shape
    D = mixed.shape[-1]
    MEMW = D - MIX
    dm = MEMW // MEM_HEADS
    NM = kv.shape[1]
    ts = _tile(S, TILE_PREFS["mem_s"], SUBLANES)
    q_off = (P - MEMW) // dm
    o_off = MIX // dm
    return pl.pallas_call(
        functools.partial(_memattn_kernel, scale=dm ** -0.5),
        grid=(B, MEM_HEADS, S // ts),
        in_specs=[pl.BlockSpec((None, ts, dm), lambda b, h, t: (b, t, q_off + h)),
                  pl.BlockSpec((None, NM, dm), lambda b, h, t: (b, 0, h)),
                  pl.BlockSpec((None, NM, dm), lambda b, h, t: (b, 0, MEM_HEADS + h)),
                  pl.BlockSpec(memory_space=pl.ANY)],
        out_specs=pl.BlockSpec((None, ts, dm), lambda b, h, t: (b, t, o_off + h)),
        out_shape=jax.ShapeDtypeStruct(mixed.shape, mixed.dtype),
        input_output_aliases={3: 0},
        compiler_params=_params("parallel", "parallel", "parallel"),
        name="memory_attention",
    )(proj, kv, kv, mixed)


def _ffn_up_kernel(h_ref, wg_ref, wv_ref, cwg_ref, cwv_ref, cbg_ref, cbv_ref, o_ref,
                   ug0_ref, uv0_ref, ug1_ref, uv1_ref, hg_ref, hv_ref, *, tm, nt, tiles_per_seq):
    s = pl.program_id(0)

    @pl.when(s == 0)
    def _():
        ug1_ref[...] = jnp.zeros_like(ug1_ref)
        uv1_ref[...] = jnp.zeros_like(uv1_ref)
        hg_ref[...] = jnp.zeros_like(hg_ref)
        hv_ref[...] = jnp.zeros_like(hv_ref)

    e = jnp.maximum(s - 1, 0)
    n = e % nt
    seq_start = ((e // nt) % tiles_per_seq) == 0

    def body(read_g, read_v, write_g, write_v):
        for u_ref, halo_ref in ((read_g, hg_ref), (read_v, hv_ref)):
            u_ref[0:SUBLANES, :] = jnp.where(seq_start, 0.0, halo_ref[n])
            halo_ref[n] = u_ref[tm:tm + SUBLANES, :]

        def conv_rows(cw_ref, cb_ref, u_ref, r0, nr):
            cw = cw_ref[...]
            return (cb_ref[...] + cw[2:3] * u_ref[SUBLANES + r0:SUBLANES + r0 + nr, :]
                    + cw[1:2] * u_ref[SUBLANES - 1 + r0:SUBLANES - 1 + r0 + nr, :]
                    + cw[0:1] * u_ref[SUBLANES - 2 + r0:SUBLANES - 2 + r0 + nr, :])

        KT = MXU_K
        n_chunks = h_ref.shape[1] // KT
        rows = tm // n_chunks
        acc = [None, None]
        for c in range(n_chunks):
            gate = conv_rows(cwg_ref, cbg_ref, read_g, c * rows, rows)
            val = conv_rows(cwv_ref, cbv_ref, read_v, c * rows, rows)
            o_ref[c * rows:(c + 1) * rows, :] = (gate * jax.nn.sigmoid(gate) * val).astype(o_ref.dtype)
            hk = h_ref[:, c * KT:(c + 1) * KT]
            for b, w_ref in enumerate((wg_ref, wv_ref)):
                part = _dot(hk, w_ref[c * KT:(c + 1) * KT, :].astype(BF16))
                acc[b] = part if acc[b] is None else acc[b] + part
        write_g[SUBLANES:SUBLANES + tm, :] = acc[0]
        write_v[SUBLANES:SUBLANES + tm, :] = acc[1]

    @pl.when(s % 2 == 0)
    def _():
        body(ug1_ref, uv1_ref, ug0_ref, uv0_ref)

    @pl.when(s % 2 == 1)
    def _():
        body(ug0_ref, uv0_ref, ug1_ref, uv1_ref)


def _ffn_up(h, w_up, conv_w, conv_b, layer, S):
    T, D = h.shape
    F = w_up.shape[-1] // 2
    tm = _tile(S, TILE_PREFS["up_m"], SUBLANES)
    tn = _tile(F, TILE_PREFS["up_n"], LANES)
    nt = F // tn
    assert conv_w.shape[1] == CONV_W and conv_b.shape[1] == 1
    n_tiles = (T // tm) * nt
    mm = lambda s: jnp.minimum(s, n_tiles - 1)
    ep = lambda s: jnp.maximum(s - 1, 0)
    u_scratch = pltpu.VMEM((tm + SUBLANES, tn), F32)
    halo_scratch = pltpu.VMEM((nt, SUBLANES, tn), F32)
    return pl.pallas_call(
        functools.partial(_ffn_up_kernel, tm=tm, nt=nt, tiles_per_seq=S // tm),
        grid=(n_tiles + 1,),
        in_specs=[pl.BlockSpec((tm, D), lambda s: (mm(s) // nt, 0)),
                  pl.BlockSpec((None, D, tn), lambda s: (layer, 0, mm(s) % nt)),
                  pl.BlockSpec((None, D, tn), lambda s: (layer, 0, nt + mm(s) % nt)),
                  pl.BlockSpec((None, CONV_W, tn), lambda s: (layer, 0, ep(s) % nt)),
                  pl.BlockSpec((None, CONV_W, tn), lambda s: (layer, 0, nt + ep(s) % nt)),
                  pl.BlockSpec((None, 1, tn), lambda s: (layer, 0, ep(s) % nt)),
                  pl.BlockSpec((None, 1, tn), lambda s: (layer, 0, nt + ep(s) % nt))],
        out_specs=pl.BlockSpec((tm, tn), lambda s: (ep(s) // nt, ep(s) % nt)),
        out_shape=jax.ShapeDtypeStruct((T, F), BF16),
        scratch_shapes=[u_scratch, u_scratch, u_scratch, u_scratch, halo_scratch, halo_scratch],
        compiler_params=_params("arbitrary"),
        name="ffn_up_conv_gate",
    )(h, w_up, w_up, conv_w, conv_w, conv_b, conv_b)


def kernel(x, mem, mem_norm, mem_w_kv, lb_logits, mix_norm, hgrn_w_in, hgrn_norm, hgrn_w_out, sb_w_in, sb_w_out,
           ffn_norm, ffn_w_up, ffn_conv_w, ffn_conv_b, ffn_w_down, final_norm):
    B, S, D = x.shape
    T = B * S
    depth, MIX = lb_logits.shape
    NM = mem.shape[1]

    hm = _rmsnorm(mem.reshape(B * NM, D), mem_norm, BF16)
    kv = _matmul(hm, mem_w_kv, BF16, name="mem_kv_proj").reshape(B, NM, -1)

    sb_cols = sb_w_in.shape[-1]
    sb_scale = jnp.where(jnp.arange(sb_cols) < MIX, LOG2E / math.sqrt(HEAD_DIM), 1.0).astype(F32)
    w_down = ffn_w_down.astype(BF16)
    conv_b = ffn_conv_b.reshape(depth, 1, -1).astype(F32)
    conv_w = ffn_conv_w.astype(F32)

    x2 = x.reshape(T, D)
    for i in range(depth):
        j = i // 2
        h = _rmsnorm(x2, mix_norm[i], BF16)
        if i % 2 == 0:
            proj = _matmul(h, hgrn_w_in, F32, layer=j, name="hgrn_in_proj").reshape(B, S, -1)
            mixed = _hgrn_mixer(proj, lb_logits, hgrn_norm[j], i, D)
            w_out = hgrn_w_out
        else:
            proj = _matmul(h, sb_w_in, BF16, layer=j, col_scale=sb_scale, name="sb_in_proj").reshape(B, S, -1)
            mixed = _sb_mixer(proj, MIX, D)
            w_out = sb_w_out
        mixed = _memattn(proj, kv, mixed, MIX)
        x2 = _matmul(mixed.reshape(T, D), w_out, F32, layer=j, residual=x2, name="out_proj")

        h = _rmsnorm(x2, ffn_norm[i], BF16)
        act = _ffn_up(h, ffn_w_up, conv_w, conv_b, i, S)
        x2 = _matmul(act, w_down, F32, layer=i, residual=x2,
                     tm_pref=TILE_PREFS["down_m"], tn_pref=TILE_PREFS["down_n"], name="ffn_down_proj")

    return _rmsnorm(x2, final_norm, F32).reshape(B, S, D)
```

```python
import functools
import math

import numpy as np
import jax
import jax.numpy as jnp
from jax import lax
from jax.experimental import pallas as pl
from jax.experimental.pallas import tpu as pltpu

F32 = jnp.float32
BF16 = jnp.bfloat16

EPS = 1e-6
F_FLOOR = 1e-20
LOG2E = 1.4426950408889634
HEAD_DIM = 128
MEM_HEADS = 4
CONV_W = 3
LANES = 128
SUBLANES = 8
MXU_K = 256
VMEM_LIMIT_BYTES = 56 * 1024 * 1024

HGRN_CHUNK = 128
SB_TILE = 512
SB_HEADS_PER_STEP = 3
TILE_PREFS = dict(norm_m=256, mm_m=1024, mm_n=512, up_m=1024, up_n=256, down_m=512, down_n=512,
                  hgrn_l=1024, mem_s=1024)


def _tile(dim, pref, quantum):
    t = min(pref, dim)
    t -= t % quantum
    while t > quantum and dim % t:
        t -= quantum
    assert t >= quantum and dim % t == 0, (dim, pref, quantum)
    return t


def _params(*sem):
    return pltpu.CompilerParams(dimension_semantics=sem, vmem_limit_bytes=VMEM_LIMIT_BYTES)


def _dot(a, b):
    return jnp.dot(a, b, preferred_element_type=F32)


def _dot_nt(a, b):
    return lax.dot_general(a, b, (((1,), (1,)), ((), ())), preferred_element_type=F32)


def _dot_tn(a, b):
    return lax.dot_general(a, b, (((0,), (0,)), ((), ())), preferred_element_type=F32)


def _rmsnorm_kernel(x_ref, g_ref, o_ref):
    x = x_ref[...]
    ms = jnp.mean(x * x, axis=-1, keepdims=True)
    o_ref[...] = (x * lax.rsqrt(ms + EPS) * g_ref[...]).astype(o_ref.dtype)


def _rmsnorm(x2d, g, out_dtype):
    T, D = x2d.shape
    tm = _tile(T, TILE_PREFS["norm_m"], SUBLANES)
    return pl.pallas_call(
        _rmsnorm_kernel,
        grid=(T // tm,),
        in_specs=[pl.BlockSpec((tm, D), lambda i: (i, 0)), pl.BlockSpec((1, D), lambda i: (0, 0))],
        out_specs=pl.BlockSpec((tm, D), lambda i: (i, 0)),
        out_shape=jax.ShapeDtypeStruct((T, D), out_dtype),
        compiler_params=_params("parallel"),
        name="rmsnorm",
    )(x2d, g.reshape(1, D).astype(F32))


def _row_tiled(rs, width):
    return jnp.tile(rs, (1, width // LANES))


def _mm_kernel(*refs, has_row_scale, has_col_scale):
    a_ref, w_ref, o_ref = refs[0], refs[1], refs[-1]
    u = _dot(a_ref[...], w_ref[...].astype(BF16))
    if has_row_scale:
        u = u * _row_tiled(refs[2][...], u.shape[1])
    if has_col_scale:
        u = u * refs[-2][...]
    o_ref[...] = u.astype(o_ref.dtype)


def _mm_res_kernel(a_ref, w_ref, r_ref, o_ref):
    o_ref[...] = r_ref[...] + _dot(a_ref[...], w_ref[...].astype(BF16))


def _mm_res_norm_kernel(a_ref, w_ref, r_ref, g_ref, o_ref, xg_ref, rs_ref, *, width):
    n = pl.program_id(1)

    @pl.when(n == 0)
    def _():
        rs_ref[...] = jnp.zeros_like(rs_ref)

    x = r_ref[...] + _dot(a_ref[...], w_ref[...].astype(BF16))
    o_ref[...] = x
    xg_ref[...] = (x * g_ref[...]).astype(xg_ref.dtype)
    xx = x * x
    rs_ref[...] += functools.reduce(jnp.add, [xx[:, c:c + LANES] for c in range(0, xx.shape[1], LANES)])

    @pl.when(n == pl.num_programs(1) - 1)
    def _():
        ms = jnp.sum(rs_ref[...], axis=-1, keepdims=True) * (1.0 / width)
        rs_ref[...] = jnp.broadcast_to(lax.rsqrt(ms + EPS), rs_ref.shape)


def _matmul(a, w, out_dtype, layer=None, residual=None, row_scale=None, col_scale=None, next_gain=None,
            tm_pref=None, tn_pref=None, name="matmul"):
    M, K = a.shape
    N = w.shape[-1]
    tm = _tile(M, tm_pref or TILE_PREFS["mm_m"], SUBLANES)
    tn = _tile(N, tn_pref or TILE_PREFS["mm_n"], LANES)
    if layer is None:
        w_spec = pl.BlockSpec((K, tn), lambda m, n: (0, n))
    else:
        w_spec = pl.BlockSpec((None, K, tn), lambda m, n: (layer, 0, n))
    row_spec = pl.BlockSpec((tm, LANES), lambda m, n: (m, 0))
    col_spec = pl.BlockSpec((1, tn), lambda m, n: (0, n))
    tile_spec = pl.BlockSpec((tm, tn), lambda m, n: (m, n))
    in_specs = [pl.BlockSpec((tm, K), lambda m, n: (m, 0)), w_spec]
    args = [a, w]
    out_specs, out_shape = tile_spec, jax.ShapeDtypeStruct((M, N), out_dtype)
    if residual is None:
        assert next_gain is None
        kern = functools.partial(_mm_kernel, has_row_scale=row_scale is not None, has_col_scale=col_scale is not None)
        if row_scale is not None:
            in_specs.append(row_spec)
            args.append(row_scale)
        if col_scale is not None:
            in_specs.append(col_spec)
            args.append(col_scale.reshape(1, N).astype(F32))
    else:
        assert row_scale is None and col_scale is None and out_dtype == F32
        kern = _mm_res_kernel
        in_specs.append(tile_spec)
        args.append(residual)
        if next_gain is not None:
            kern = functools.partial(_mm_res_norm_kernel, width=N)
            in_specs.append(col_spec)
            args.append(next_gain.reshape(1, N).astype(F32))
            out_specs = [tile_spec, tile_spec, row_spec]
            out_shape = [out_shape, jax.ShapeDtypeStruct((M, N), BF16), jax.ShapeDtypeStruct((M, LANES), F32)]
    return pl.pallas_call(
        kern,
        grid=(M // tm, N // tn),
        in_specs=in_specs,
        out_specs=out_specs,
        out_shape=out_shape,
        compiler_params=_params("parallel", "arbitrary"),
        name=name,
    )(*args)


def _hgrn_consts(C):
    nlev = int(math.log2(C))
    assert 1 << nlev == C
    r = np.arange(C)
    rr, jj = r[:, None], r[None, :]
    blocks = []
    for l in range(nlev):
        c = 1 << l
        ref = ((r & ~(2 * c - 1)) + c - 1)[:, None]
        odd = ((r & c) != 0)[:, None]
        blocks.append(np.where(odd, (jj > ref) & (jj <= rr), (jj > rr) & (jj <= ref)))
    blocks.append(jj <= rr)
    blocks.append(jj > rr)
    wall = np.concatenate(blocks, axis=0).astype(np.float32)
    x = rr ^ jj
    lvl = np.where(x == 0, -1, np.floor(np.log2(np.maximum(x, 1))).astype(np.int64))
    lvl = np.where(jj > rr, -2, lvl).astype(np.int32)
    return wall, lvl, nlev


def _hgrn_kernel(q_ref, zf_ref, v_ref, zg_ref, lbl_ref, gn_ref, wall_ref, lvl_ref, o_ref, st_ref,
                 *, layer, C, nlev, n_chunks):
    @pl.when(pl.program_id(2) == 0)
    def _():
        st_ref[...] = jnp.zeros_like(st_ref)

    lbl = lbl_ref[...]
    e = jnp.exp(lbl - jnp.max(lbl, axis=0, keepdims=True))
    p = e / jnp.sum(e, axis=0, keepdims=True)
    cs = p[0:1]
    for j in range(1, layer + 1):
        cs = cs + p[j:j + 1]
    lb = jnp.maximum(cs - p[0:1], 0.0)
    gn = gn_ref[...]
    lvl = lvl_ref[...]

    sts = [st_ref[0], st_ref[1]]
    for c in range(n_chunks):
        rows = slice(c * C, (c + 1) * C)
        zf = zf_ref[rows, :]
        f = lb + (1.0 - lb) * jax.nn.sigmoid(zf)
        g = jnp.log(jnp.maximum(f, F_FLOOR))
        kin2 = (1.0 - lb) * jax.nn.sigmoid(-zf)
        g_hi = g.astype(BF16)
        g_lo = (g - g_hi.astype(F32)).astype(BF16)
        d2 = _dot(wall_ref[...], jnp.concatenate([g_hi, g_lo], axis=0))
        og2 = []
        for hd in range(2):
            head = slice(hd * HEAD_DIM, (hd + 1) * HEAD_DIM)
            d = d2[:, head]
            q = q_ref[rows, head]
            v = v_ref[rows, head].astype(BF16)
            kin = kin2[:, head]
            scores = jnp.where(lvl == -1, _dot_nt(q.astype(BF16), kin.astype(BF16)), 0.0)
            for l in range(nlev):
                el = jnp.exp(d[l * C:(l + 1) * C])
                pl_ = _dot_nt((q * el).astype(BF16), (kin * el).astype(BF16))
                scores = jnp.where(lvl == l, pl_, scores)
            b = d[nlev * C:(nlev + 1) * C]
            b_rest = d[(nlev + 1) * C:(nlev + 2) * C]

            st = sts[hd]
            inter = _dot_nt((q * jnp.exp(b)).astype(BF16), st.astype(BF16))
            intra = _dot(scores.astype(BF16), v)
            sts[hd] = st * jnp.exp(b[C - 1:C, :]) + _dot_tn(v, (kin * jnp.exp(b_rest)).astype(BF16))

            og = (inter + intra) * jax.nn.sigmoid(zg_ref[rows, head])
            ms = jnp.mean(og * og, axis=-1, keepdims=True)
            og2.append(og * lax.rsqrt(ms + EPS))
        o_ref[rows, :] = (jnp.concatenate(og2, axis=1) * gn).astype(o_ref.dtype)
    st_ref[0] = sts[0]
    st_ref[1] = sts[1]


def _hgrn_mixer(proj, lb_logits, gnorm, layer, D):
    B, S, _ = proj.shape
    depth, MIX = lb_logits.shape
    W = 2 * HEAD_DIM
    HP = MIX // W
    assert MIX % W == 0
    C = min(HGRN_CHUNK, S)
    L = _tile(S, max(TILE_PREFS["hgrn_l"], C), C)
    wall, lvl, nlev = _hgrn_consts(C)
    wall = np.concatenate([wall, wall], axis=1)
    blk = lambda off: pl.BlockSpec((None, L, W), lambda b, h, c: (b, c, off + h))
    return pl.pallas_call(
        functools.partial(_hgrn_kernel, layer=layer, C=C, nlev=nlev, n_chunks=L // C),
        grid=(B, HP, S // L),
        in_specs=[blk(0), blk(HP), blk(2 * HP), blk(3 * HP),
                  pl.BlockSpec((depth, W), lambda b, h, c: (0, h)),
                  pl.BlockSpec((1, W), lambda b, h, c: (0, h)),
                  pl.BlockSpec(wall.shape, lambda b, h, c: (0, 0)),
                  pl.BlockSpec(lvl.shape, lambda b, h, c: (0, 0))],
        out_specs=pl.BlockSpec((None, L, W), lambda b, h, c: (b, c, h)),
        out_shape=jax.ShapeDtypeStruct((B, S, D), BF16),
        scratch_shapes=[pltpu.VMEM((2, HEAD_DIM, HEAD_DIM), F32)],
        compiler_params=_params("parallel", "parallel", "arbitrary"),
        name="hgrn2_mixer",
    )(proj, proj, proj, proj, lb_logits.astype(F32), gnorm.reshape(1, MIX).astype(F32),
      jnp.asarray(wall, BF16), jnp.asarray(lvl))


def _sb_consts():
    r = np.arange(LANES)
    u = np.concatenate([(r[:, None] > r[None, :]), np.ones((LANES, LANES), bool)], axis=1)
    return np.concatenate([u, u], axis=0).astype(np.float32)


def _sb_kernel(q_ref, k_ref, v_ref, u_ref, o_ref, z_ref, w_ref, *, T, G):
    i = pl.program_id(2)
    uu = u_ref[...]

    def rows_of(j):
        return slice(j * T, (j + 1) * T) if isinstance(j, int) else pl.ds(pl.multiple_of(j * T, T), T)

    def lanes_of(g):
        return slice(g * HEAD_DIM, (g + 1) * HEAD_DIM)

    def scores(g, j):
        return _dot_nt(q_ref[:, lanes_of(g)], k_ref[rows_of(j), lanes_of(g)])

    def weighted_values(g, w, j):
        return _dot(w, v_ref[rows_of(j), lanes_of(g)])

    def weights(z, carry, diagonal):
        neg_abs = lax.bitcast_convert_type(lax.bitcast_convert_type(z, jnp.uint32) | jnp.uint32(0x80000000), F32)
        sp = jnp.maximum(z, 0.0) + jnp.log(1.0 + jnp.exp2(neg_abs)) * LOG2E
        log_beta = z - sp
        if diagonal:
            strictly_earlier = (lax.broadcasted_iota(jnp.int32, (T, T), 0)
                                > lax.broadcasted_iota(jnp.int32, (T, T), 1))
            sp = jnp.where(strictly_earlier, sp, 0.0)
        hi = sp.astype(BF16)
        lo = (sp - hi.astype(F32)).astype(BF16)
        ws = [None] * (T // LANES)
        for c in reversed(range(T // LANES)):
            seg = slice(c * LANES, (c + 1) * LANES)
            cs = _dot(jnp.concatenate([hi[:, seg], lo[:, seg]], axis=1), uu)
            w = jnp.exp2(log_beta[:, seg] - cs[:, :LANES] - carry)
            if diagonal:
                w = jnp.where(strictly_earlier[:, seg], w, 0.0)
            ws[c] = w.astype(BF16)
            carry = carry + cs[:, LANES:]
        return jnp.concatenate(ws, axis=1), carry

    def first(g):
        w_ref[g], carry = weights(scores(g, i), jnp.zeros((T, LANES), F32), True)
        z_ref[0, g] = scores(g, jnp.maximum(i - 1, 0))
        return carry, jnp.zeros((T, HEAD_DIM), F32)

    def step(g, j, slot, carry, acc):
        acc = acc + weighted_values(g, w_ref[g], j + 1)
        w, carry = weights(z_ref[slot, g], carry, False)
        z_ref[1 - slot, g] = scores(g, jnp.maximum(j - 1, 0))
        w_ref[g] = w
        return carry, acc

    def two_tiles(p, state):
        j = i - 1 - 2 * p
        state = tuple(step(g, j, 0, *state[g]) for g in range(G))
        return tuple(step(g, j - 1, 1, *state[g]) for g in range(G))

    def last_tile(_, state):
        return tuple(step(g, 0, 0, *state[g]) for g in range(G))

    state = tuple(first(g) for g in range(G))
    state = lax.fori_loop(0, i // 2, two_tiles, state)
    state = lax.fori_loop(0, i % 2, last_tile, state)
    for g in range(G):
        o_ref[:, lanes_of(g)] = (state[g][1] + weighted_values(g, w_ref[g], 0)).astype(o_ref.dtype)


def _sb_mixer(proj, MIX, D):
    B, S, _ = proj.shape
    H = MIX // HEAD_DIM
    T = _tile(S, SB_TILE, LANES)
    G = SB_HEADS_PER_STEP
    assert H % G == 0
    HG = H // G
    W = G * HEAD_DIM
    u = _sb_consts()
    return pl.pallas_call(
        functools.partial(_sb_kernel, T=T, G=G),
        grid=(B, HG, S // T),
        in_specs=[pl.BlockSpec((None, T, W), lambda b, h, i: (b, i, h)),
                  pl.BlockSpec((None, S, W), lambda b, h, i: (b, 0, HG + h)),
                  pl.BlockSpec((None, S, W), lambda b, h, i: (b, 0, 2 * HG + h)),
                  pl.BlockSpec(u.shape, lambda b, h, i: (0, 0))],
        out_specs=pl.BlockSpec((None, T, W), lambda b, h, i: (b, i, h)),
        out_shape=jax.ShapeDtypeStruct((B, S, D), BF16),
        scratch_shapes=[pltpu.VMEM((2, G, T, T), F32), pltpu.VMEM((G, T, T), BF16)],
        compiler_params=_params("parallel", "parallel", "arbitrary"),
        name="stickbreaking_mixer",
    )(proj, proj, proj, jnp.asarray(u, BF16))


def _memattn_kernel(q_ref, mk_ref, mv_ref, mixed_ref, o_ref, *, scale):
    del mixed_ref
    s = _dot_nt(q_ref[...].astype(BF16), mk_ref[...]) * scale
    e = jnp.exp(s - jnp.max(s, axis=-1, keepdims=True))
    p = e / jnp.sum(e, axis=-1, keepdims=True)
    o_ref[...] = _dot(p.astype(BF16), mv_ref[...]).astype(o_ref.dtype)


def _memattn(proj, kv, mixed, MIX):
    B, S, P = proj.shape
    D = mixed.shape[-1]
    MEMW = D - MIX
    dm = MEMW // MEM_HEADS
    NM = kv.shape[1]
    ts = _tile(S, TILE_PREFS["mem_s"], SUBLANES)
    q_off = (P - MEMW) // dm
    o_off = MIX // dm
    return pl.pallas_call(
        functools.partial(_memattn_kernel, scale=dm ** -0.5),
        grid=(B, MEM_HEADS, S // ts),
        in_specs=[pl.BlockSpec((None, ts, dm), lambda b, h, t: (b, t, q_off + h)),
                  pl.BlockSpec((None, NM, dm), lambda b, h, t: (b, 0, h)),
                  pl.BlockSpec((None, NM, dm), lambda b, h, t: (b, 0, MEM_HEADS + h)),
                  pl.BlockSpec(memory_space=pl.ANY)],
        out_specs=pl.BlockSpec((None, ts, dm), lambda b, h, t: (b, t, o_off + h)),
        out_shape=jax.ShapeDtypeStruct(mixed.shape, mixed.dtype),
        input_output_aliases={3: 0},
        compiler_params=_params("parallel", "parallel", "parallel"),
        name="memory_attention",
    )(proj, kv, kv, mixed)


def _ffn_up_kernel(h_ref, rs_ref, wg_ref, wv_ref, cwg_ref, cwv_ref, cbg_ref, cbv_ref, o_ref,
                   ug0_ref, uv0_ref, ug1_ref, uv1_ref, hg_ref, hv_ref, *, tm, nt, tiles_per_seq):
    s = pl.program_id(0)

    @pl.when(s == 0)
    def _():
        ug1_ref[...] = jnp.zeros_like(ug1_ref)
        uv1_ref[...] = jnp.zeros_like(uv1_ref)
        hg_ref[...] = jnp.zeros_like(hg_ref)
        hv_ref[...] = jnp.zeros_like(hv_ref)

    e = jnp.maximum(s - 1, 0)
    n = e % nt
    seq_start = ((e // nt) % tiles_per_seq) == 0

    def body(read_g, read_v, write_g, write_v):
        for u_ref, halo_ref in ((read_g, hg_ref), (read_v, hv_ref)):
            u_ref[0:SUBLANES, :] = jnp.where(seq_start, 0.0, halo_ref[n])
            halo_ref[n] = u_ref[tm:tm + SUBLANES, :]

        def conv_rows(cw_ref, cb_ref, u_ref, r0, nr):
            cw = cw_ref[...]
            return (cb_ref[...] + cw[2:3] * u_ref[SUBLANES + r0:SUBLANES + r0 + nr, :]
                    + cw[1:2] * u_ref[SUBLANES - 1 + r0:SUBLANES - 1 + r0 + nr, :]
                    + cw[0:1] * u_ref[SUBLANES - 2 + r0:SUBLANES - 2 + r0 + nr, :])

        KT = MXU_K
        n_chunks = h_ref.shape[1] // KT
        rows = tm // n_chunks
        acc = [None, None]
        for c in range(n_chunks):
            gate = conv_rows(cwg_ref, cbg_ref, read_g, c * rows, rows)
            val = conv_rows(cwv_ref, cbv_ref, read_v, c * rows, rows)
            o_ref[c * rows:(c + 1) * rows, :] = (gate * jax.nn.sigmoid(gate) * val).astype(o_ref.dtype)
            hk = h_ref[:, c * KT:(c + 1) * KT]
            for b, w_ref in enumerate((wg_ref, wv_ref)):
                part = _dot(hk, w_ref[c * KT:(c + 1) * KT, :].astype(BF16))
                acc[b] = part if acc[b] is None else acc[b] + part
        rs = _row_tiled(rs_ref[...], o_ref.shape[1])
        write_g[SUBLANES:SUBLANES + tm, :] = acc[0] * rs
        write_v[SUBLANES:SUBLANES + tm, :] = acc[1] * rs

    @pl.when(s % 2 == 0)
    def _():
        body(ug1_ref, uv1_ref, ug0_ref, uv0_ref)

    @pl.when(s % 2 == 1)
    def _():
        body(ug0_ref, uv0_ref, ug1_ref, uv1_ref)


def _ffn_up(h, row_scale, w_up, conv_w, conv_b, layer, S):
    T, D = h.shape
    F = w_up.shape[-1] // 2
    tm = _tile(S, TILE_PREFS["up_m"], SUBLANES)
    tn = _tile(F, TILE_PREFS["up_n"], LANES)
    nt = F // tn
    assert conv_w.shape[1] == CONV_W and conv_b.shape[1] == 1
    n_tiles = (T // tm) * nt
    mm = lambda s: jnp.minimum(s, n_tiles - 1)
    ep = lambda s: jnp.maximum(s - 1, 0)
    u_scratch = pltpu.VMEM((tm + SUBLANES, tn), F32)
    halo_scratch = pltpu.VMEM((nt, SUBLANES, tn), F32)
    return pl.pallas_call(
        functools.partial(_ffn_up_kernel, tm=tm, nt=nt, tiles_per_seq=S // tm),
        grid=(n_tiles + 1,),
        in_specs=[pl.BlockSpec((tm, D), lambda s: (mm(s) // nt, 0)),
                  pl.BlockSpec((tm, LANES), lambda s: (mm(s) // nt, 0)),
                  pl.BlockSpec((None, D, tn), lambda s: (layer, 0, mm(s) % nt)),
                  pl.BlockSpec((None, D, tn), lambda s: (layer, 0, nt + mm(s) % nt)),
                  pl.BlockSpec((None, CONV_W, tn), lambda s: (layer, 0, ep(s) % nt)),
                  pl.BlockSpec((None, CONV_W, tn), lambda s: (layer, 0, nt + ep(s) % nt)),
                  pl.BlockSpec((None, 1, tn), lambda s: (layer, 0, ep(s) % nt)),
                  pl.BlockSpec((None, 1, tn), lambda s: (layer, 0, nt + ep(s) % nt))],
        out_specs=pl.BlockSpec((tm, tn), lambda s: (ep(s) // nt, ep(s) % nt)),
        out_shape=jax.ShapeDtypeStruct((T, F), BF16),
        scratch_shapes=[u_scratch, u_scratch, u_scratch, u_scratch, halo_scratch, halo_scratch],
        compiler_params=_params("arbitrary"),
        name="ffn_up_conv_gate",
    )(h, row_scale, w_up, w_up, conv_w, conv_w, conv_b, conv_b)


def kernel(x, mem, mem_norm, mem_w_kv, lb_logits, mix_norm, hgrn_w_in, hgrn_norm, hgrn_w_out, sb_w_in, sb_w_out,
           ffn_norm, ffn_w_up, ffn_conv_w, ffn_conv_b, ffn_w_down, final_norm):
    B, S, D = x.shape
    T = B * S
    depth, MIX = lb_logits.shape
    NM = mem.shape[1]

    hm = _rmsnorm(mem.reshape(B * NM, D), mem_norm, BF16)
    kv = _matmul(hm, mem_w_kv, BF16, name="mem_kv_proj").reshape(B, NM, -1)

    sb_cols = sb_w_in.shape[-1]
    sb_scale = jnp.where(jnp.arange(sb_cols) < MIX, LOG2E / math.sqrt(HEAD_DIM), 1.0).astype(F32)
    w_down = ffn_w_down.astype(BF16)
    conv_b = ffn_conv_b.reshape(depth, 1, -1).astype(F32)
    conv_w = ffn_conv_w.astype(F32)

    x2 = x.reshape(T, D)
    h = _rmsnorm(x2, mix_norm[0], BF16)
    row_scale = None
    for i in range(depth):
        j = i // 2
        if i % 2 == 0:
            proj = _matmul(h, hgrn_w_in, F32, layer=j, row_scale=row_scale, name="hgrn_in_proj").reshape(B, S, -1)
            mixed = _hgrn_mixer(proj, lb_logits, hgrn_norm[j], i, D)
            w_out = hgrn_w_out
        else:
            proj = _matmul(h, sb_w_in, BF16, layer=j, row_scale=row_scale, col_scale=sb_scale,
                           name="sb_in_proj").reshape(B, S, -1)
            mixed = _sb_mixer(proj, MIX, D)
            w_out = sb_w_out
        mixed = _memattn(proj, kv, mixed, MIX)
        x2, h, row_scale = _matmul(mixed.reshape(T, D), w_out, F32, layer=j, residual=x2, next_gain=ffn_norm[i],
                                   name="out_proj")

        act = _ffn_up(h, row_scale, ffn_w_up, conv_w, conv_b, i, S)
        down = dict(layer=i, residual=x2, tm_pref=TILE_PREFS["down_m"], tn_pref=TILE_PREFS["down_n"],
                    name="ffn_down_proj")
        if i + 1 < depth:
            x2, h, row_scale = _matmul(act, w_down, F32, next_gain=mix_norm[i + 1], **down)
        else:
            x2 = _matmul(act, w_down, F32, **down)

    return _rmsnorm(x2, final_norm, F32).reshape(B, S, D)
```

```python
import functools
import math

import numpy as np
import jax
import jax.numpy as jnp
from jax import lax
from jax.experimental import pallas as pl
from jax.experimental.pallas import tpu as pltpu

F32 = jnp.float32
BF16 = jnp.bfloat16

EPS = 1e-6
F_FLOOR = 1e-20
LOG2E = 1.4426950408889634
HEAD_DIM = 128
MEM_HEADS = 4
CONV_W = 3
LANES = 128
SUBLANES = 8
MXU_K = 256
VMEM_LIMIT_BYTES = 56 * 1024 * 1024

HGRN_CHUNK = 128
SB_TILE = 512
SB_HEADS_PER_STEP = 3
TILE_PREFS = dict(norm_m=256, mm_m=1024, mm_n=512, up_m=1024, up_n=256, down_m=512, down_n=512,
                  hgrn_l=2048, mem_s=1024)


def _tile(dim, pref, quantum):
    t = min(pref, dim)
    t -= t % quantum
    while t > quantum and dim % t:
        t -= quantum
    assert t >= quantum and dim % t == 0, (dim, pref, quantum)
    return t


def _params(*sem):
    return pltpu.CompilerParams(dimension_semantics=sem, vmem_limit_bytes=VMEM_LIMIT_BYTES)


def _dot(a, b):
    return jnp.dot(a, b, preferred_element_type=F32)


def _dot_nt(a, b):
    return lax.dot_general(a, b, (((1,), (1,)), ((), ())), preferred_element_type=F32)


def _dot_tn(a, b):
    return lax.dot_general(a, b, (((0,), (0,)), ((), ())), preferred_element_type=F32)


def _rmsnorm_kernel(x_ref, g_ref, o_ref):
    x = x_ref[...]
    ms = jnp.mean(x * x, axis=-1, keepdims=True)
    o_ref[...] = (x * lax.rsqrt(ms + EPS) * g_ref[...]).astype(o_ref.dtype)


def _rmsnorm(x2d, g, out_dtype):
    T, D = x2d.shape
    tm = _tile(T, TILE_PREFS["norm_m"], SUBLANES)
    return pl.pallas_call(
        _rmsnorm_kernel,
        grid=(T // tm,),
        in_specs=[pl.BlockSpec((tm, D), lambda i: (i, 0)), pl.BlockSpec((1, D), lambda i: (0, 0))],
        out_specs=pl.BlockSpec((tm, D), lambda i: (i, 0)),
        out_shape=jax.ShapeDtypeStruct((T, D), out_dtype),
        compiler_params=_params("parallel"),
        name="rmsnorm",
    )(x2d, g.reshape(1, D).astype(F32))


def _row_tiled(rs, width):
    return jnp.tile(rs, (1, width // LANES))


def _mm_kernel(*refs, has_row_scale, has_col_scale):
    a_ref, w_ref, o_ref = refs[0], refs[1], refs[-1]
    u = _dot(a_ref[...], w_ref[...].astype(BF16))
    if has_row_scale:
        u = u * _row_tiled(refs[2][...], u.shape[1])
    if has_col_scale:
        u = u * refs[-2][...]
    o_ref[...] = u.astype(o_ref.dtype)


def _mm_res_kernel(a_ref, w_ref, r_ref, *refs, nt, width, with_norm):
    if with_norm:
        g_ref, o_ref, xg_ref, rs_ref, raw0_ref, raw1_ref = refs
    else:
        o_ref, raw0_ref, raw1_ref = refs
    s = pl.program_id(0)
    n = jnp.maximum(s - 1, 0) % nt
    tm, tn = o_ref.shape
    K = a_ref.shape[1]
    k_chunks = K // MXU_K
    row_chunks = min(k_chunks, tm // (2 * SUBLANES))
    rows = tm // row_chunks

    @pl.when(s == 0)
    def _():
        raw1_ref[...] = jnp.zeros_like(raw1_ref)

    if with_norm:
        @pl.when(n == 0)
        def _():
            rs_ref[...] = jnp.zeros_like(rs_ref)

    def finish_rows(read_ref, c):
        rr = slice(c * rows, (c + 1) * rows)
        x = r_ref[rr, :] + read_ref[rr, :]
        o_ref[rr, :] = x
        if with_norm:
            xg_ref[rr, :] = (x * g_ref[...]).astype(xg_ref.dtype)
            xx = x * x
            rs_ref[rr, :] += functools.reduce(jnp.add, [xx[:, l:l + LANES] for l in range(0, tn, LANES)])

    def body(read_ref, write_ref):
        acc = None
        done = 0
        for c in range(k_chunks):
            while done * k_chunks < (c + 1) * row_chunks and done < row_chunks:
                finish_rows(read_ref, done)
                done += 1
            part = _dot(a_ref[:, c * MXU_K:(c + 1) * MXU_K], w_ref[c * MXU_K:(c + 1) * MXU_K, :].astype(BF16))
            acc = part if acc is None else acc + part
        write_ref[...] = acc

    @pl.when(s % 2 == 0)
    def _():
        body(raw1_ref, raw0_ref)

    @pl.when(s % 2 == 1)
    def _():
        body(raw0_ref, raw1_ref)

    if with_norm:
        @pl.when(n == nt - 1)
        def _():
            ms = jnp.sum(rs_ref[...], axis=-1, keepdims=True) * (1.0 / width)
            rs_ref[...] = jnp.broadcast_to(lax.rsqrt(ms + EPS), rs_ref.shape)


def _matmul_residual(a, w, layer, residual, next_gain=None, tm_pref=None, tn_pref=None, name="matmul_residual"):
    M, K = a.shape
    N = w.shape[-1]
    tm = _tile(M, tm_pref or TILE_PREFS["mm_m"], SUBLANES)
    tn = _tile(N, tn_pref or TILE_PREFS["mm_n"], LANES)
    nt = N // tn
    n_tiles = (M // tm) * nt
    assert K % MXU_K == 0
    mm = lambda s: jnp.minimum(s, n_tiles - 1)
    ep = lambda s: jnp.maximum(s - 1, 0)
    tile_spec = pl.BlockSpec((tm, tn), lambda s: (ep(s) // nt, ep(s) % nt))
    in_specs = [pl.BlockSpec((tm, K), lambda s: (mm(s) // nt, 0)),
                pl.BlockSpec((None, K, tn), lambda s: (layer, 0, mm(s) % nt)),
                tile_spec]
    args = [a, w, residual]
    out_specs, out_shape = tile_spec, jax.ShapeDtypeStruct((M, N), F32)
    if next_gain is not None:
        in_specs.append(pl.BlockSpec((1, tn), lambda s: (0, ep(s) % nt)))
        args.append(next_gain.reshape(1, N).astype(F32))
        out_specs = [tile_spec, tile_spec, pl.BlockSpec((tm, LANES), lambda s: (ep(s) // nt, 0))]
        out_shape = [out_shape, jax.ShapeDtypeStruct((M, N), BF16), jax.ShapeDtypeStruct((M, LANES), F32)]
    return pl.pallas_call(
        functools.partial(_mm_res_kernel, nt=nt, width=N, with_norm=next_gain is not None),
        grid=(n_tiles + 1,),
        in_specs=in_specs,
        out_specs=out_specs,
        out_shape=out_shape,
        scratch_shapes=[pltpu.VMEM((tm, tn), F32), pltpu.VMEM((tm, tn), F32)],
        compiler_params=_params("arbitrary"),
        name=name,
    )(*args)


def _matmul(a, w, out_dtype, layer=None, row_scale=None, col_scale=None, tm_pref=None, tn_pref=None, name="matmul"):
    M, K = a.shape
    N = w.shape[-1]
    tm = _tile(M, tm_pref or TILE_PREFS["mm_m"], SUBLANES)
    tn = _tile(N, tn_pref or TILE_PREFS["mm_n"], LANES)
    if layer is None:
        w_spec = pl.BlockSpec((K, tn), lambda m, n: (0, n))
    else:
        w_spec = pl.BlockSpec((None, K, tn), lambda m, n: (layer, 0, n))
    in_specs = [pl.BlockSpec((tm, K), lambda m, n: (m, 0)), w_spec]
    args = [a, w]
    if row_scale is not None:
        in_specs.append(pl.BlockSpec((tm, LANES), lambda m, n: (m, 0)))
        args.append(row_scale)
    if col_scale is not None:
        in_specs.append(pl.BlockSpec((1, tn), lambda m, n: (0, n)))
        args.append(col_scale.reshape(1, N).astype(F32))
    return pl.pallas_call(
        functools.partial(_mm_kernel, has_row_scale=row_scale is not None, has_col_scale=col_scale is not None),
        grid=(M // tm, N // tn),
        in_specs=in_specs,
        out_specs=pl.BlockSpec((tm, tn), lambda m, n: (m, n)),
        out_shape=jax.ShapeDtypeStruct((M, N), out_dtype),
        compiler_params=_params("parallel", "arbitrary"),
        name=name,
    )(*args)


def _hgrn_consts(C):
    nlev = int(math.log2(C))
    assert 1 << nlev == C
    r = np.arange(C)
    rr, jj = r[:, None], r[None, :]
    blocks = []
    for l in range(nlev):
        c = 1 << l
        ref = ((r & ~(2 * c - 1)) + c - 1)[:, None]
        odd = ((r & c) != 0)[:, None]
        blocks.append(np.where(odd, (jj > ref) & (jj <= rr), (jj > rr) & (jj <= ref)))
    blocks.append(jj <= rr)
    blocks.append(jj > rr)
    wall = np.concatenate(blocks, axis=0).astype(np.float32)
    x = rr ^ jj
    lvl = np.where(x == 0, -1, np.floor(np.log2(np.maximum(x, 1))).astype(np.int64))
    lvl = np.where(jj > rr, -2, lvl).astype(np.int32)
    return wall, lvl, nlev


def _hgrn_kernel(q_ref, zf_ref, v_ref, zg_ref, lbl_ref, gn_ref, wall_ref, lvl_ref, o_ref, st_ref,
                 *, layer, C, nlev, n_chunks):
    @pl.when(pl.program_id(2) == 0)
    def _():
        st_ref[...] = jnp.zeros_like(st_ref)

    lbl = lbl_ref[...]
    e = jnp.exp(lbl - jnp.max(lbl, axis=0, keepdims=True))
    p = e / jnp.sum(e, axis=0, keepdims=True)
    cs = p[0:1]
    for j in range(1, layer + 1):
        cs = cs + p[j:j + 1]
    lb = jnp.maximum(cs - p[0:1], 0.0)
    gn = gn_ref[...]
    lvl = lvl_ref[...]

    sts = [st_ref[0], st_ref[1]]
    for c in range(n_chunks):
        rows = slice(c * C, (c + 1) * C)
        zf = zf_ref[rows, :]
        f = lb + (1.0 - lb) * jax.nn.sigmoid(zf)
        g = jnp.log(jnp.maximum(f, F_FLOOR))
        kin2 = (1.0 - lb) * jax.nn.sigmoid(-zf)
        g_hi = g.astype(BF16)
        g_lo = (g - g_hi.astype(F32)).astype(BF16)
        d2 = _dot(wall_ref[...], jnp.concatenate([g_hi, g_lo], axis=0))
        og2 = []
        for hd in range(2):
            head = slice(hd * HEAD_DIM, (hd + 1) * HEAD_DIM)
            d = d2[:, head]
            q = q_ref[rows, head]
            v = v_ref[rows, head].astype(BF16)
            kin = kin2[:, head]
            scores = jnp.where(lvl == -1, _dot_nt(q.astype(BF16), kin.astype(BF16)), 0.0)
            for l in range(nlev):
                el = jnp.exp(d[l * C:(l + 1) * C])
                pl_ = _dot_nt((q * el).astype(BF16), (kin * el).astype(BF16))
                scores = jnp.where(lvl == l, pl_, scores)
            b = d[nlev * C:(nlev + 1) * C]
            b_rest = d[(nlev + 1) * C:(nlev + 2) * C]

            st = sts[hd]
            inter = _dot_nt((q * jnp.exp(b)).astype(BF16), st.astype(BF16))
            intra = _dot(scores.astype(BF16), v)
            sts[hd] = st * jnp.exp(b[C - 1:C, :]) + _dot_tn(v, (kin * jnp.exp(b_rest)).astype(BF16))

            og = (inter + intra) * jax.nn.sigmoid(zg_ref[rows, head])
            ms = jnp.mean(og * og, axis=-1, keepdims=True)
            og2.append(og * lax.rsqrt(ms + EPS))
        o_ref[rows, :] = (jnp.concatenate(og2, axis=1) * gn).astype(o_ref.dtype)
    st_ref[0] = sts[0]
    st_ref[1] = sts[1]


def _hgrn_mixer(proj, lb_logits, gnorm, layer, D):
    B, S, _ = proj.shape
    depth, MIX = lb_logits.shape
    W = 2 * HEAD_DIM
    HP = MIX // W
    assert MIX % W == 0
    C = min(HGRN_CHUNK, S)
    L = _tile(S, max(TILE_PREFS["hgrn_l"], C), C)
    wall, lvl, nlev = _hgrn_consts(C)
    wall = np.concatenate([wall, wall], axis=1)
    blk = lambda off: pl.BlockSpec((None, L, W), lambda b, h, c: (b, c, off + h))
    return pl.pallas_call(
        functools.partial(_hgrn_kernel, layer=layer, C=C, nlev=nlev, n_chunks=L // C),
        grid=(B, HP, S // L),
        in_specs=[blk(0), blk(HP), blk(2 * HP), blk(3 * HP),
                  pl.BlockSpec((depth, W), lambda b, h, c: (0, h)),
                  pl.BlockSpec((1, W), lambda b, h, c: (0, h)),
                  pl.BlockSpec(wall.shape, lambda b, h, c: (0, 0)),
                  pl.BlockSpec(lvl.shape, lambda b, h, c: (0, 0))],
        out_specs=pl.BlockSpec((None, L, W), lambda b, h, c: (b, c, h)),
        out_shape=jax.ShapeDtypeStruct((B, S, D), BF16),
        scratch_shapes=[pltpu.VMEM((2, HEAD_DIM, HEAD_DIM), F32)],
        compiler_params=_params("parallel", "parallel", "arbitrary"),
        name="hgrn2_mixer",
    )(proj, proj, proj, proj, lb_logits.astype(F32), gnorm.reshape(1, MIX).astype(F32),
      jnp.asarray(wall, BF16), jnp.asarray(lvl))


def _sb_consts():
    r = np.arange(LANES)
    u = np.concatenate([(r[:, None] > r[None, :]), np.ones((LANES, LANES), bool)], axis=1)
    return np.concatenate([u, u], axis=0).astype(np.float32)


def _sb_kernel(q_ref, k_ref, v_ref, u_ref, o_ref, z_ref, w_ref, *, T, G):
    i = pl.program_id(2)
    uu = u_ref[...]

    def rows_of(j):
        return slice(j * T, (j + 1) * T) if isinstance(j, int) else pl.ds(pl.multiple_of(j * T, T), T)

    def lanes_of(g):
        return slice(g * HEAD_DIM, (g + 1) * HEAD_DIM)

    def scores(g, j):
        return _dot_nt(q_ref[:, lanes_of(g)], k_ref[rows_of(j), lanes_of(g)])

    def weighted_values(g, w, j):
        return _dot(w, v_ref[rows_of(j), lanes_of(g)])

    def weights(z, carry, diagonal):
        neg_abs = lax.bitcast_convert_type(lax.bitcast_convert_type(z, jnp.uint32) | jnp.uint32(0x80000000), F32)
        sp = jnp.maximum(z, 0.0) + jnp.log(1.0 + jnp.exp2(neg_abs)) * LOG2E
        log_beta = z - sp
        if diagonal:
            strictly_earlier = (lax.broadcasted_iota(jnp.int32, (T, T), 0)
                                > lax.broadcasted_iota(jnp.int32, (T, T), 1))
            sp = jnp.where(strictly_earlier, sp, 0.0)
        hi = sp.astype(BF16)
        lo = (sp - hi.astype(F32)).astype(BF16)
        ws = [None] * (T // LANES)
        for c in reversed(range(T // LANES)):
            seg = slice(c * LANES, (c + 1) * LANES)
            cs = _dot(jnp.concatenate([hi[:, seg], lo[:, seg]], axis=1), uu)
            w = jnp.exp2(log_beta[:, seg] - cs[:, :LANES] - carry)
            if diagonal:
                w = jnp.where(strictly_earlier[:, seg], w, 0.0)
            ws[c] = w.astype(BF16)
            carry = carry + cs[:, LANES:]
        return jnp.concatenate(ws, axis=1), carry

    def first(g):
        w_ref[g], carry = weights(scores(g, i), jnp.zeros((T, LANES), F32), True)
        z_ref[0, g] = scores(g, jnp.maximum(i - 1, 0))
        return carry, jnp.zeros((T, HEAD_DIM), F32)

    def step(g, j, slot, carry, acc):
        acc = acc + weighted_values(g, w_ref[g], j + 1)
        w, carry = weights(z_ref[slot, g], carry, False)
        z_ref[1 - slot, g] = scores(g, jnp.maximum(j - 1, 0))
        w_ref[g] = w
        return carry, acc

    def two_tiles(p, state):
        j = i - 1 - 2 * p
        state = tuple(step(g, j, 0, *state[g]) for g in range(G))
        return tuple(step(g, j - 1, 1, *state[g]) for g in range(G))

    def last_tile(_, state):
        return tuple(step(g, 0, 0, *state[g]) for g in range(G))

    state = tuple(first(g) for g in range(G))
    state = lax.fori_loop(0, i // 2, two_tiles, state)
    state = lax.fori_loop(0, i % 2, last_tile, state)
    for g in range(G):
        o_ref[:, lanes_of(g)] = (state[g][1] + weighted_values(g, w_ref[g], 0)).astype(o_ref.dtype)


def _sb_mixer(proj, MIX, D):
    B, S, _ = proj.shape
    H = MIX // HEAD_DIM
    T = _tile(S, SB_TILE, LANES)
    G = SB_HEADS_PER_STEP
    assert H % G == 0
    HG = H // G
    W = G * HEAD_DIM
    u = _sb_consts()
    return pl.pallas_call(
        functools.partial(_sb_kernel, T=T, G=G),
        grid=(B, HG, S // T),
        in_specs=[pl.BlockSpec((None, T, W), lambda b, h, i: (b, i, h)),
                  pl.BlockSpec((None, S, W), lambda b, h, i: (b, 0, HG + h)),
                  pl.BlockSpec((None, S, W), lambda b, h, i: (b, 0, 2 * HG + h)),
                  pl.BlockSpec(u.shape, lambda b, h, i: (0, 0))],
        out_specs=pl.BlockSpec((None, T, W), lambda b, h, i: (b, i, h)),
        out_shape=jax.ShapeDtypeStruct((B, S, D), BF16),
        scratch_shapes=[pltpu.VMEM((2, G, T, T), F32), pltpu.VMEM((G, T, T), BF16)],
        compiler_params=_params("parallel", "parallel", "arbitrary"),
        name="stickbreaking_mixer",
    )(proj, proj, proj, jnp.asarray(u, BF16))


def _memattn_kernel(q_ref, mk_ref, mv_ref, mixed_ref, o_ref, *, scale):
    del mixed_ref
    s = _dot_nt(q_ref[...].astype(BF16), mk_ref[...]) * scale
    e = jnp.exp(s - jnp.max(s, axis=-1, keepdims=True))
    p = e / jnp.sum(e, axis=-1, keepdims=True)
    o_ref[...] = _dot(p.astype(BF16), mv_ref[...]).astype(o_ref.dtype)


def _memattn(proj, kv, mixed, MIX):
    B, S, P = proj.shape
    D = mixed.shape[-1]
    MEMW = D - MIX
    dm = MEMW // MEM_HEADS
    NM = kv.shape[1]
    ts = _tile(S, TILE_PREFS["mem_s"], SUBLANES)
    q_off = (P - MEMW) // dm
    o_off = MIX // dm
    return pl.pallas_call(
        functools.partial(_memattn_kernel, scale=dm ** -0.5),
        grid=(B, MEM_HEADS, S // ts),
        in_specs=[pl.BlockSpec((None, ts, dm), lambda b, h, t: (b, t, q_off + h)),
                  pl.BlockSpec((None, NM, dm), lambda b, h, t: (b, 0, h)),
                  pl.BlockSpec((None, NM, dm), lambda b, h, t: (b, 0, MEM_HEADS + h)),
                  pl.BlockSpec(memory_space=pl.ANY)],
        out_specs=pl.BlockSpec((None, ts, dm), lambda b, h, t: (b, t, o_off + h)),
        out_shape=jax.ShapeDtypeStruct(mixed.shape, mixed.dtype),
        input_output_aliases={3: 0},
        compiler_params=_params("parallel", "parallel", "parallel"),
        name="memory_attention",
    )(proj, kv, kv, mixed)


def _ffn_up_kernel(h_ref, rs_ref, wg_ref, wv_ref, cwg_ref, cwv_ref, cbg_ref, cbv_ref, o_ref,
                   ug0_ref, uv0_ref, ug1_ref, uv1_ref, hg_ref, hv_ref, *, tm, nt, tiles_per_seq):
    s = pl.program_id(0)

    @pl.when(s == 0)
    def _():
        ug1_ref[...] = jnp.zeros_like(ug1_ref)
        uv1_ref[...] = jnp.zeros_like(uv1_ref)
        hg_ref[...] = jnp.zeros_like(hg_ref)
        hv_ref[...] = jnp.zeros_like(hv_ref)

    e = jnp.maximum(s - 1, 0)
    n = e % nt
    seq_start = ((e // nt) % tiles_per_seq) == 0

    def body(read_g, read_v, write_g, write_v):
        for u_ref, halo_ref in ((read_g, hg_ref), (read_v, hv_ref)):
            u_ref[0:SUBLANES, :] = jnp.where(seq_start, 0.0, halo_ref[n])
            halo_ref[n] = u_ref[tm:tm + SUBLANES, :]

        def conv_rows(cw_ref, cb_ref, u_ref, r0, nr):
            cw = cw_ref[...]
            return (cb_ref[...] + cw[2:3] * u_ref[SUBLANES + r0:SUBLANES + r0 + nr, :]
                    + cw[1:2] * u_ref[SUBLANES - 1 + r0:SUBLANES - 1 + r0 + nr, :]
                    + cw[0:1] * u_ref[SUBLANES - 2 + r0:SUBLANES - 2 + r0 + nr, :])

        KT = MXU_K
        n_chunks = h_ref.shape[1] // KT
        rows = tm // n_chunks
        acc = [None, None]
        for c in range(n_chunks):
            gate = conv_rows(cwg_ref, cbg_ref, read_g, c * rows, rows)
            val = conv_rows(cwv_ref, cbv_ref, read_v, c * rows, rows)
            o_ref[c * rows:(c + 1) * rows, :] = (gate * jax.nn.sigmoid(gate) * val).astype(o_ref.dtype)
            hk = h_ref[:, c * KT:(c + 1) * KT]
            for b, w_ref in enumerate((wg_ref, wv_ref)):
                part = _dot(hk, w_ref[c * KT:(c + 1) * KT, :].astype(BF16))
                acc[b] = part if acc[b] is None else acc[b] + part
        rs = _row_tiled(rs_ref[...], o_ref.shape[1])
        write_g[SUBLANES:SUBLANES + tm, :] = acc[0] * rs
        write_v[SUBLANES:SUBLANES + tm, :] = acc[1] * rs

    @pl.when(s % 2 == 0)
    def _():
        body(ug1_ref, uv1_ref, ug0_ref, uv0_ref)

    @pl.when(s % 2 == 1)
    def _():
        body(ug0_ref, uv0_ref, ug1_ref, uv1_ref)


def _ffn_up(h, row_scale, w_up, conv_w, conv_b, layer, S):
    T, D = h.shape
    F = w_up.shape[-1] // 2
    tm = _tile(S, TILE_PREFS["up_m"], SUBLANES)
    tn = _tile(F, TILE_PREFS["up_n"], LANES)
    nt = F // tn
    assert conv_w.shape[1] == CONV_W and conv_b.shape[1] == 1
    n_tiles = (T // tm) * nt
    mm = lambda s: jnp.minimum(s, n_tiles - 1)
    ep = lambda s: jnp.maximum(s - 1, 0)
    u_scratch = pltpu.VMEM((tm + SUBLANES, tn), F32)
    halo_scratch = pltpu.VMEM((nt, SUBLANES, tn), F32)
    return pl.pallas_call(
        functools.partial(_ffn_up_kernel, tm=tm, nt=nt, tiles_per_seq=S // tm),
        grid=(n_tiles + 1,),
        in_specs=[pl.BlockSpec((tm, D), lambda s: (mm(s) // nt, 0)),
                  pl.BlockSpec((tm, LANES), lambda s: (mm(s) // nt, 0)),
                  pl.BlockSpec((None, D, tn), lambda s: (layer, 0, mm(s) % nt)),
                  pl.BlockSpec((None, D, tn), lambda s: (layer, 0, nt + mm(s) % nt)),
                  pl.BlockSpec((None, CONV_W, tn), lambda s: (layer, 0, ep(s) % nt)),
                  pl.BlockSpec((None, CONV_W, tn), lambda s: (layer, 0, nt + ep(s) % nt)),
                  pl.BlockSpec((None, 1, tn), lambda s: (layer, 0, ep(s) % nt)),
                  pl.BlockSpec((None, 1, tn), lambda s: (layer, 0, nt + ep(s) % nt))],
        out_specs=pl.BlockSpec((tm, tn), lambda s: (ep(s) // nt, ep(s) % nt)),
        out_shape=jax.ShapeDtypeStruct((T, F), BF16),
        scratch_shapes=[u_scratch, u_scratch, u_scratch, u_scratch, halo_scratch, halo_scratch],
        compiler_params=_params("arbitrary"),
        name="ffn_up_conv_gate",
    )(h, row_scale, w_up, w_up, conv_w, conv_w, conv_b, conv_b)


def kernel(x, mem, mem_norm, mem_w_kv, lb_logits, mix_norm, hgrn_w_in, hgrn_norm, hgrn_w_out, sb_w_in, sb_w_out,
           ffn_norm, ffn_w_up, ffn_conv_w, ffn_conv_b, ffn_w_down, final_norm):
    B, S, D = x.shape
    T = B * S
    depth, MIX = lb_logits.shape
    NM = mem.shape[1]

    hm = _rmsnorm(mem.reshape(B * NM, D), mem_norm, BF16)
    kv = _matmul(hm, mem_w_kv, BF16, name="mem_kv_proj").reshape(B, NM, -1)

    sb_cols = sb_w_in.shape[-1]
    sb_scale = jnp.where(jnp.arange(sb_cols) < MIX, LOG2E / math.sqrt(HEAD_DIM), 1.0).astype(F32)
    w_down = ffn_w_down.astype(BF16)
    conv_b = ffn_conv_b.reshape(depth, 1, -1).astype(F32)
    conv_w = ffn_conv_w.astype(F32)

    x2 = x.reshape(T, D)
    h = _rmsnorm(x2, mix_norm[0], BF16)
    row_scale = None
    for i in range(depth):
        j = i // 2
        if i % 2 == 0:
            proj = _matmul(h, hgrn_w_in, F32, layer=j, row_scale=row_scale, name="hgrn_in_proj").reshape(B, S, -1)
            mixed = _hgrn_mixer(proj, lb_logits, hgrn_norm[j], i, D)
            w_out = hgrn_w_out
        else:
            proj = _matmul(h, sb_w_in, BF16, layer=j, row_scale=row_scale, col_scale=sb_scale,
                           name="sb_in_proj").reshape(B, S, -1)
            mixed = _sb_mixer(proj, MIX, D)
            w_out = sb_w_out
        mixed = _memattn(proj, kv, mixed, MIX)
        x2, h, row_scale = _matmul_residual(mixed.reshape(T, D), w_out, j, x2, next_gain=ffn_norm[i], name="out_proj")

        act = _ffn_up(h, row_scale, ffn_w_up, conv_w, conv_b, i, S)
        down = dict(tm_pref=TILE_PREFS["down_m"], tn_pref=TILE_PREFS["down_n"], name="ffn_down_proj")
        if i + 1 < depth:
            x2, h, row_scale = _matmul_residual(act, w_down, i, x2, next_gain=mix_norm[i + 1], **down)
        else:
            x2 = _matmul_residual(act, w_down, i, x2, **down)

    return _rmsnorm(x2, final_norm, F32).reshape(B, S, D)
```

```python
import functools
import math

import numpy as np
import jax
import jax.numpy as jnp
from jax import lax
from jax.experimental import pallas as pl
from jax.experimental.pallas import tpu as pltpu

F32 = jnp.float32
BF16 = jnp.bfloat16

EPS = 1e-6
F_FLOOR = 1e-20
LOG2E = 1.4426950408889634
HEAD_DIM = 128
MEM_HEADS = 4
CONV_W = 3
LANES = 128
SUBLANES = 8
MXU_K = 256
VMEM_LIMIT_BYTES = 56 * 1024 * 1024

HGRN_CHUNK = 128
SB_TILE = 512
SB_HEADS_PER_STEP = 3
TILE_PREFS = dict(norm_m=256, mm_m=1024, mm_n=512, up_m=1024, up_n=256, down_m=512, down_n=512,
                  hgrn_l=2048, mem_s=1024)


def _tile(dim, pref, quantum):
    t = min(pref, dim)
    t -= t % quantum
    while t > quantum and dim % t:
        t -= quantum
    assert t >= quantum and dim % t == 0, (dim, pref, quantum)
    return t


def _params(*sem):
    return pltpu.CompilerParams(dimension_semantics=sem, vmem_limit_bytes=VMEM_LIMIT_BYTES)


def _dot(a, b):
    return jnp.dot(a, b, preferred_element_type=F32)


def _dot_nt(a, b):
    return lax.dot_general(a, b, (((1,), (1,)), ((), ())), preferred_element_type=F32)


def _dot_tn(a, b):
    return lax.dot_general(a, b, (((0,), (0,)), ((), ())), preferred_element_type=F32)


def _rmsnorm_kernel(x_ref, g_ref, o_ref):
    x = x_ref[...]
    ms = jnp.mean(x * x, axis=-1, keepdims=True)
    o_ref[...] = (x * lax.rsqrt(ms + EPS) * g_ref[...]).astype(o_ref.dtype)


def _rmsnorm(x2d, g, out_dtype):
    T, D = x2d.shape
    tm = _tile(T, TILE_PREFS["norm_m"], SUBLANES)
    return pl.pallas_call(
        _rmsnorm_kernel,
        grid=(T // tm,),
        in_specs=[pl.BlockSpec((tm, D), lambda i: (i, 0)), pl.BlockSpec((1, D), lambda i: (0, 0))],
        out_specs=pl.BlockSpec((tm, D), lambda i: (i, 0)),
        out_shape=jax.ShapeDtypeStruct((T, D), out_dtype),
        compiler_params=_params("parallel"),
        name="rmsnorm",
    )(x2d, g.reshape(1, D).astype(F32))


def _row_tiled(rs, width):
    return jnp.tile(rs, (1, width // LANES))


def _mm_kernel(*refs, has_row_scale, has_col_scale):
    a_ref, w_ref, o_ref = refs[0], refs[1], refs[-1]
    u = _dot(a_ref[...], w_ref[...].astype(BF16))
    if has_row_scale:
        u = u * _row_tiled(refs[2][...], u.shape[1])
    if has_col_scale:
        u = u * refs[-2][...]
    o_ref[...] = u.astype(o_ref.dtype)


def _mm_res_kernel(a_ref, w_ref, r_ref, o_ref):
    o_ref[...] = r_ref[...] + _dot(a_ref[...], w_ref[...].astype(BF16))


def _mm_res_norm_kernel(a_ref, w_ref, r_ref, g_ref, o_ref, xg_ref, rs_ref, *, width):
    n = pl.program_id(1)

    @pl.when(n == 0)
    def _():
        rs_ref[...] = jnp.zeros_like(rs_ref)

    x = r_ref[...] + _dot(a_ref[...], w_ref[...].astype(BF16))
    o_ref[...] = x
    xg_ref[...] = (x * g_ref[...]).astype(xg_ref.dtype)
    xx = x * x
    rs_ref[...] += functools.reduce(jnp.add, [xx[:, c:c + LANES] for c in range(0, xx.shape[1], LANES)])

    @pl.when(n == pl.num_programs(1) - 1)
    def _():
        ms = jnp.sum(rs_ref[...], axis=-1, keepdims=True) * (1.0 / width)
        rs_ref[...] = jnp.broadcast_to(lax.rsqrt(ms + EPS), rs_ref.shape)


def _matmul(a, w, out_dtype, layer=None, residual=None, row_scale=None, col_scale=None, next_gain=None,
            tm_pref=None, tn_pref=None, name="matmul"):
    M, K = a.shape
    N = w.shape[-1]
    tm = _tile(M, tm_pref or TILE_PREFS["mm_m"], SUBLANES)
    tn = _tile(N, tn_pref or TILE_PREFS["mm_n"], LANES)
    if layer is None:
        w_spec = pl.BlockSpec((K, tn), lambda m, n: (0, n))
    else:
        w_spec = pl.BlockSpec((None, K, tn), lambda m, n: (layer, 0, n))
    row_spec = pl.BlockSpec((tm, LANES), lambda m, n: (m, 0))
    col_spec = pl.BlockSpec((1, tn), lambda m, n: (0, n))
    tile_spec = pl.BlockSpec((tm, tn), lambda m, n: (m, n))
    in_specs = [pl.BlockSpec((tm, K), lambda m, n: (m, 0)), w_spec]
    args = [a, w]
    out_specs, out_shape = tile_spec, jax.ShapeDtypeStruct((M, N), out_dtype)
    if residual is None:
        assert next_gain is None
        kern = functools.partial(_mm_kernel, has_row_scale=row_scale is not None, has_col_scale=col_scale is not None)
        if row_scale is not None:
            in_specs.append(row_spec)
            args.append(row_scale)
        if col_scale is not None:
            in_specs.append(col_spec)
            args.append(col_scale.reshape(1, N).astype(F32))
    else:
        assert row_scale is None and col_scale is None and out_dtype == F32
        kern = _mm_res_kernel
        in_specs.append(tile_spec)
        args.append(residual)
        if next_gain is not None:
            kern = functools.partial(_mm_res_norm_kernel, width=N)
            in_specs.append(col_spec)
            args.append(next_gain.reshape(1, N).astype(F32))
            out_specs = [tile_spec, tile_spec, row_spec]
            out_shape = [out_shape, jax.ShapeDtypeStruct((M, N), BF16), jax.ShapeDtypeStruct((M, LANES), F32)]
    return pl.pallas_call(
        kern,
        grid=(M // tm, N // tn),
        in_specs=in_specs,
        out_specs=out_specs,
        out_shape=out_shape,
        compiler_params=_params("parallel", "arbitrary"),
        name=name,
    )(*args)


def _hgrn_consts(C):
    nlev = int(math.log2(C))
    assert 1 << nlev == C
    r = np.arange(C)
    rr, jj = r[:, None], r[None, :]
    blocks = []
    for l in range(nlev):
        c = 1 << l
        ref = ((r & ~(2 * c - 1)) + c - 1)[:, None]
        odd = ((r & c) != 0)[:, None]
        blocks.append(np.where(odd, (jj > ref) & (jj <= rr), (jj > rr) & (jj <= ref)))
    blocks.append(jj <= rr)
    blocks.append(jj > rr)
    wall = np.concatenate(blocks, axis=0).astype(np.float32)
    x = rr ^ jj
    lvl = np.where(x == 0, -1, np.floor(np.log2(np.maximum(x, 1))).astype(np.int64))
    lvl = np.where(jj > rr, -2, lvl).astype(np.int32)
    return wall, lvl, nlev


def _hgrn_kernel(q_ref, zf_ref, v_ref, zg_ref, lbl_ref, gn_ref, wall_ref, lvl_ref, o_ref, st_ref,
                 *, layer, C, nlev, n_chunks):
    @pl.when(pl.program_id(2) == 0)
    def _():
        st_ref[...] = jnp.zeros_like(st_ref)

    lbl = lbl_ref[...]
    e = jnp.exp(lbl - jnp.max(lbl, axis=0, keepdims=True))
    p = e / jnp.sum(e, axis=0, keepdims=True)
    cs = p[0:1]
    for j in range(1, layer + 1):
        cs = cs + p[j:j + 1]
    lb = jnp.maximum(cs - p[0:1], 0.0)
    gn = gn_ref[...]
    lvl = lvl_ref[...]

    sts = [st_ref[0], st_ref[1]]
    for c in range(n_chunks):
        rows = slice(c * C, (c + 1) * C)
        zf = zf_ref[rows, :]
        f = lb + (1.0 - lb) * jax.nn.sigmoid(zf)
        g = jnp.log(jnp.maximum(f, F_FLOOR))
        kin2 = (1.0 - lb) * jax.nn.sigmoid(-zf)
        g_hi = g.astype(BF16)
        g_lo = (g - g_hi.astype(F32)).astype(BF16)
        d2 = _dot(wall_ref[...], jnp.concatenate([g_hi, g_lo], axis=0))
        og2 = []
        for hd in range(2):
            head = slice(hd * HEAD_DIM, (hd + 1) * HEAD_DIM)
            d = d2[:, head]
            q = q_ref[rows, head]
            v = v_ref[rows, head].astype(BF16)
            kin = kin2[:, head]
            scores = jnp.where(lvl == -1, _dot_nt(q.astype(BF16), kin.astype(BF16)), 0.0)
            for l in range(nlev):
                el = jnp.exp(d[l * C:(l + 1) * C])
                pl_ = _dot_nt((q * el).astype(BF16), (kin * el).astype(BF16))
                scores = jnp.where(lvl == l, pl_, scores)
            b = d[nlev * C:(nlev + 1) * C]
            b_rest = d[(nlev + 1) * C:(nlev + 2) * C]

            st = sts[hd]
            inter = _dot_nt((q * jnp.exp(b)).astype(BF16), st.astype(BF16))
            intra = _dot(scores.astype(BF16), v)
            sts[hd] = st * jnp.exp(b[C - 1:C, :]) + _dot_tn(v, (kin * jnp.exp(b_rest)).astype(BF16))

            og = (inter + intra) * jax.nn.sigmoid(zg_ref[rows, head])
            ms = jnp.mean(og * og, axis=-1, keepdims=True)
            og2.append(og * lax.rsqrt(ms + EPS))
        o_ref[rows, :] = (jnp.concatenate(og2, axis=1) * gn).astype(o_ref.dtype)
    st_ref[0] = sts[0]
    st_ref[1] = sts[1]


def _hgrn_mixer(proj, lb_logits, gnorm, layer, D):
    B, S, _ = proj.shape
    depth, MIX = lb_logits.shape
    W = 2 * HEAD_DIM
    HP = MIX // W
    assert MIX % W == 0
    C = min(HGRN_CHUNK, S)
    L = _tile(S, max(TILE_PREFS["hgrn_l"], C), C)
    wall, lvl, nlev = _hgrn_consts(C)
    wall = np.concatenate([wall, wall], axis=1)
    blk = lambda off: pl.BlockSpec((None, L, W), lambda b, h, c: (b, c, off + h))
    return pl.pallas_call(
        functools.partial(_hgrn_kernel, layer=layer, C=C, nlev=nlev, n_chunks=L // C),
        grid=(B, HP, S // L),
        in_specs=[blk(0), blk(HP), blk(2 * HP), blk(3 * HP),
                  pl.BlockSpec((depth, W), lambda b, h, c: (0, h)),
                  pl.BlockSpec((1, W), lambda b, h, c: (0, h)),
                  pl.BlockSpec(wall.shape, lambda b, h, c: (0, 0)),
                  pl.BlockSpec(lvl.shape, lambda b, h, c: (0, 0))],
        out_specs=pl.BlockSpec((None, L, W), lambda b, h, c: (b, c, h)),
        out_shape=jax.ShapeDtypeStruct((B, S, D), BF16),
        scratch_shapes=[pltpu.VMEM((2, HEAD_DIM, HEAD_DIM), F32)],
        compiler_params=_params("parallel", "parallel", "arbitrary"),
        name="hgrn2_mixer",
    )(proj, proj, proj, proj, lb_logits.astype(F32), gnorm.reshape(1, MIX).astype(F32),
      jnp.asarray(wall, BF16), jnp.asarray(lvl))


def _sb_consts():
    r = np.arange(LANES)
    u = np.concatenate([(r[:, None] > r[None, :]), np.ones((LANES, LANES), bool)], axis=1)
    return np.concatenate([u, u], axis=0).astype(np.float32)


def _sb_kernel(q_ref, k_ref, v_ref, u_ref, o_ref, z_ref, w_ref, *, T, G):
    i = pl.program_id(2)
    uu = u_ref[...]

    def rows_of(j):
        return slice(j * T, (j + 1) * T) if isinstance(j, int) else pl.ds(pl.multiple_of(j * T, T), T)

    def lanes_of(g):
        return slice(g * HEAD_DIM, (g + 1) * HEAD_DIM)

    def scores(g, j):
        return _dot_nt(q_ref[:, lanes_of(g)], k_ref[rows_of(j), lanes_of(g)])

    def weighted_values(g, w, j):
        return _dot(w, v_ref[rows_of(j), lanes_of(g)])

    def weights(z, carry, diagonal):
        neg_abs = lax.bitcast_convert_type(lax.bitcast_convert_type(z, jnp.uint32) | jnp.uint32(0x80000000), F32)
        sp = jnp.maximum(z, 0.0) + jnp.log(1.0 + jnp.exp2(neg_abs)) * LOG2E
        log_beta = z - sp
        if diagonal:
            strictly_earlier = (lax.broadcasted_iota(jnp.int32, (T, T), 0)
                                > lax.broadcasted_iota(jnp.int32, (T, T), 1))
            sp = jnp.where(strictly_earlier, sp, 0.0)
        hi = sp.astype(BF16)
        lo = (sp - hi.astype(F32)).astype(BF16)
        ws = [None] * (T // LANES)
        for c in reversed(range(T // LANES)):
            seg = slice(c * LANES, (c + 1) * LANES)
            cs = _dot(jnp.concatenate([hi[:, seg], lo[:, seg]], axis=1), uu)
            w = jnp.exp2(log_beta[:, seg] - cs[:, :LANES] - carry)
            if diagonal:
                w = jnp.where(strictly_earlier[:, seg], w, 0.0)
            ws[c] = w.astype(BF16)
            carry = carry + cs[:, LANES:]
        return jnp.concatenate(ws, axis=1), carry

    def first(g):
        w_ref[g], carry = weights(scores(g, i), jnp.zeros((T, LANES), F32), True)
        z_ref[0, g] = scores(g, jnp.maximum(i - 1, 0))
        return carry, jnp.zeros((T, HEAD_DIM), F32)

    def step(g, j, slot, carry, acc):
        acc = acc + weighted_values(g, w_ref[g], j + 1)
        w, carry = weights(z_ref[slot, g], carry, False)
        z_ref[1 - slot, g] = scores(g, jnp.maximum(j - 1, 0))
        w_ref[g] = w
        return carry, acc

    def two_tiles(p, state):
        j = i - 1 - 2 * p
        state = tuple(step(g, j, 0, *state[g]) for g in range(G))
        return tuple(step(g, j - 1, 1, *state[g]) for g in range(G))

    def last_tile(_, state):
        return tuple(step(g, 0, 0, *state[g]) for g in range(G))

    state = tuple(first(g) for g in range(G))
    state = lax.fori_loop(0, i // 2, two_tiles, state)
    state = lax.fori_loop(0, i % 2, last_tile, state)
    for g in range(G):
        o_ref[:, lanes_of(g)] = (state[g][1] + weighted_values(g, w_ref[g], 0)).astype(o_ref.dtype)


def _sb_mixer(proj, MIX, D):
    B, S, _ = proj.shape
    H = MIX // HEAD_DIM
    T = _tile(S, SB_TILE, LANES)
    G = SB_HEADS_PER_STEP
    assert H % G == 0
    HG = H // G
    W = G * HEAD_DIM
    u = _sb_consts()
    return pl.pallas_call(
        functools.partial(_sb_kernel, T=T, G=G),
        grid=(B, HG, S // T),
        in_specs=[pl.BlockSpec((None, T, W), lambda b, h, i: (b, i, h)),
                  pl.BlockSpec((None, S, W), lambda b, h, i: (b, 0, HG + h)),
                  pl.BlockSpec((None, S, W), lambda b, h, i: (b, 0, 2 * HG + h)),
                  pl.BlockSpec(u.shape, lambda b, h, i: (0, 0))],
        out_specs=pl.BlockSpec((None, T, W), lambda b, h, i: (b, i, h)),
        out_shape=jax.ShapeDtypeStruct((B, S, D), BF16),
        scratch_shapes=[pltpu.VMEM((2, G, T, T), F32), pltpu.VMEM((G, T, T), BF16)],
        compiler_params=_params("parallel", "parallel", "arbitrary"),
        name="stickbreaking_mixer",
    )(proj, proj, proj, jnp.asarray(u, BF16))


def _memattn_kernel(q_ref, mk_ref, mv_ref, mixed_ref, o_ref, *, scale):
    del mixed_ref
    s = _dot_nt(q_ref[...].astype(BF16), mk_ref[...]) * scale
    e = jnp.exp(s - jnp.max(s, axis=-1, keepdims=True))
    p = e / jnp.sum(e, axis=-1, keepdims=True)
    o_ref[...] = _dot(p.astype(BF16), mv_ref[...]).astype(o_ref.dtype)


def _memattn(proj, kv, mixed, MIX):
    B, S, P = proj.shape
    D = mixed.shape[-1]
    MEMW = D - MIX
    dm = MEMW // MEM_HEADS
    NM = kv.shape[1]
    ts = _tile(S, TILE_PREFS["mem_s"], SUBLANES)
    q_off = (P - MEMW) // dm
    o_off = MIX // dm
    return pl.pallas_call(
        functools.partial(_memattn_kernel, scale=dm ** -0.5),
        grid=(B, MEM_HEADS, S // ts),
        in_specs=[pl.BlockSpec((None, ts, dm), lambda b, h, t: (b, t, q_off + h)),
                  pl.BlockSpec((None, NM, dm), lambda b, h, t: (b, 0, h)),
                  pl.BlockSpec((None, NM, dm), lambda b, h, t: (b, 0, MEM_HEADS + h)),
                  pl.BlockSpec(memory_space=pl.ANY)],
        out_specs=pl.BlockSpec((None, ts, dm), lambda b, h, t: (b, t, o_off + h)),
        out_shape=jax.ShapeDtypeStruct(mixed.shape, mixed.dtype),
        input_output_aliases={3: 0},
        compiler_params=_params("parallel", "parallel", "parallel"),
        name="memory_attention",
    )(proj, kv, kv, mixed)


def _ffn_up_kernel(h_ref, rs_ref, wg_ref, wv_ref, cwg_ref, cwv_ref, cbg_ref, cbv_ref, o_ref,
                   ug0_ref, uv0_ref, ug1_ref, uv1_ref, hg_ref, hv_ref, *, tm, nt, tiles_per_seq):
    s = pl.program_id(0)

    @pl.when(s == 0)
    def _():
        ug1_ref[...] = jnp.zeros_like(ug1_ref)
        uv1_ref[...] = jnp.zeros_like(uv1_ref)
        hg_ref[...] = jnp.zeros_like(hg_ref)
        hv_ref[...] = jnp.zeros_like(hv_ref)

    e = jnp.maximum(s - 1, 0)
    n = e % nt
    seq_start = ((e // nt) % tiles_per_seq) == 0

    def body(read_g, read_v, write_g, write_v):
        for u_ref, halo_ref in ((read_g, hg_ref), (read_v, hv_ref)):
            u_ref[0:SUBLANES, :] = jnp.where(seq_start, 0.0, halo_ref[n])
            halo_ref[n] = u_ref[tm:tm + SUBLANES, :]

        def conv_rows(cw_ref, cb_ref, u_ref, r0, nr):
            cw = cw_ref[...]
            return (cb_ref[...] + cw[2:3] * u_ref[SUBLANES + r0:SUBLANES + r0 + nr, :]
                    + cw[1:2] * u_ref[SUBLANES - 1 + r0:SUBLANES - 1 + r0 + nr, :]
                    + cw[0:1] * u_ref[SUBLANES - 2 + r0:SUBLANES - 2 + r0 + nr, :])

        KT = MXU_K
        n_chunks = h_ref.shape[1] // KT
        rows = tm // n_chunks
        acc = [None, None]
        for c in range(n_chunks):
            gate = conv_rows(cwg_ref, cbg_ref, read_g, c * rows, rows)
            val = conv_rows(cwv_ref, cbv_ref, read_v, c * rows, rows)
            o_ref[c * rows:(c + 1) * rows, :] = (gate * jax.nn.sigmoid(gate) * val).astype(o_ref.dtype)
            hk = h_ref[:, c * KT:(c + 1) * KT]
            for b, w_ref in enumerate((wg_ref, wv_ref)):
                part = _dot(hk, w_ref[c * KT:(c + 1) * KT, :].astype(BF16))
                acc[b] = part if acc[b] is None else acc[b] + part
        rs = _row_tiled(rs_ref[...], o_ref.shape[1])
        write_g[SUBLANES:SUBLANES + tm, :] = acc[0] * rs
        write_v[SUBLANES:SUBLANES + tm, :] = acc[1] * rs

    @pl.when(s % 2 == 0)
    def _():
        body(ug1_ref, uv1_ref, ug0_ref, uv0_ref)

    @pl.when(s % 2 == 1)
    def _():
        body(ug0_ref, uv0_ref, ug1_ref, uv1_ref)


def _ffn_up(h, row_scale, w_up, conv_w, conv_b, layer, S):
    T, D = h.shape
    F = w_up.shape[-1] // 2
    tm = _tile(S, TILE_PREFS["up_m"], SUBLANES)
    tn = _tile(F, TILE_PREFS["up_n"], LANES)
    nt = F // tn
    assert conv_w.shape[1] == CONV_W and conv_b.shape[1] == 1
    n_tiles = (T // tm) * nt
    mm = lambda s: jnp.minimum(s, n_tiles - 1)
    ep = lambda s: jnp.maximum(s - 1, 0)
    u_scratch = pltpu.VMEM((tm + SUBLANES, tn), F32)
    halo_scratch = pltpu.VMEM((nt, SUBLANES, tn), F32)
    return pl.pallas_call(
        functools.partial(_ffn_up_kernel, tm=tm, nt=nt, tiles_per_seq=S // tm),
        grid=(n_tiles + 1,),
        in_specs=[pl.BlockSpec((tm, D), lambda s: (mm(s) // nt, 0)),
                  pl.BlockSpec((tm, LANES), lambda s: (mm(s) // nt, 0)),
                  pl.BlockSpec((None, D, tn), lambda s: (layer, 0, mm(s) % nt)),
                  pl.BlockSpec((None, D, tn), lambda s: (layer, 0, nt + mm(s) % nt)),
                  pl.BlockSpec((None, CONV_W, tn), lambda s: (layer, 0, ep(s) % nt)),
                  pl.BlockSpec((None, CONV_W, tn), lambda s: (layer, 0, nt + ep(s) % nt)),
                  pl.BlockSpec((None, 1, tn), lambda s: (layer, 0, ep(s) % nt)),
                  pl.BlockSpec((None, 1, tn), lambda s: (layer, 0, nt + ep(s) % nt))],
        out_specs=pl.BlockSpec((tm, tn), lambda s: (ep(s) // nt, ep(s) % nt)),
        out_shape=jax.ShapeDtypeStruct((T, F), BF16),
        scratch_shapes=[u_scratch, u_scratch, u_scratch, u_scratch, halo_scratch, halo_scratch],
        compiler_params=_params("arbitrary"),
        name="ffn_up_conv_gate",
    )(h, row_scale, w_up, w_up, conv_w, conv_w, conv_b, conv_b)


def kernel(x, mem, mem_norm, mem_w_kv, lb_logits, mix_norm, hgrn_w_in, hgrn_norm, hgrn_w_out, sb_w_in, sb_w_out,
           ffn_norm, ffn_w_up, ffn_conv_w, ffn_conv_b, ffn_w_down, final_norm):
    B, S, D = x.shape
    T = B * S
    depth, MIX = lb_logits.shape
    NM = mem.shape[1]

    hm = _rmsnorm(mem.reshape(B * NM, D), mem_norm, BF16)
    kv = _matmul(hm, mem_w_kv, BF16, name="mem_kv_proj").reshape(B, NM, -1)

    sb_cols = sb_w_in.shape[-1]
    sb_scale = jnp.where(jnp.arange(sb_cols) < MIX, LOG2E / math.sqrt(HEAD_DIM), 1.0).astype(F32)
    w_down = ffn_w_down.astype(BF16)
    conv_b = ffn_conv_b.reshape(depth, 1, -1).astype(F32)
    conv_w = ffn_conv_w.astype(F32)

    x2 = x.reshape(T, D)
    h = _rmsnorm(x2, mix_norm[0], BF16)
    row_scale = None
    for i in range(depth):
        j = i // 2
        if i % 2 == 0:
            proj = _matmul(h, hgrn_w_in, F32, layer=j, row_scale=row_scale, name="hgrn_in_proj").reshape(B, S, -1)
            mixed = _hgrn_mixer(proj, lb_logits, hgrn_norm[j], i, D)
            w_out = hgrn_w_out
        else:
            proj = _matmul(h, sb_w_in, BF16, layer=j, row_scale=row_scale, col_scale=sb_scale,
                           name="sb_in_proj").reshape(B, S, -1)
            mixed = _sb_mixer(proj, MIX, D)
            w_out = sb_w_out
        mixed = _memattn(proj, kv, mixed, MIX)
        x2, h, row_scale = _matmul(mixed.reshape(T, D), w_out, F32, layer=j, residual=x2, next_gain=ffn_norm[i],
                                   name="out_proj")

        act = _ffn_up(h, row_scale, ffn_w_up, conv_w, conv_b, i, S)
        down = dict(layer=i, residual=x2, tm_pref=TILE_PREFS["down_m"], tn_pref=TILE_PREFS["down_n"],
                    name="ffn_down_proj")
        if i + 1 < depth:
            x2, h, row_scale = _matmul(act, w_down, F32, next_gain=mix_norm[i + 1], **down)
        else:
            x2 = _matmul(act, w_down, F32, **down)

    return _rmsnorm(x2, final_norm, F32).reshape(B, S, D)
```

```python
import functools
import math

import numpy as np
import jax
import jax.numpy as jnp
from jax import lax
from jax.experimental import pallas as pl
from jax.experimental.pallas import tpu as pltpu

F32 = jnp.float32
BF16 = jnp.bfloat16

EPS = 1e-6
F_FLOOR = 1e-20
LOG2E = 1.4426950408889634
HEAD_DIM = 128
MEM_HEADS = 4
CONV_W = 3
LANES = 128
SUBLANES = 8
MXU_K = 256
VMEM_LIMIT_BYTES = 56 * 1024 * 1024

HGRN_CHUNK = 128
SB_TILE = 512
SB_HEADS_PER_STEP = 3
TILE_PREFS = dict(norm_m=256, mm_m=1024, mm_n=512, up_m=1024, up_n=256, down_m=512, down_n=512,
                  hgrn_l=4096, mem_s=2048)


def _tile(dim, pref, quantum):
    t = min(pref, dim)
    t -= t % quantum
    while t > quantum and dim % t:
        t -= quantum
    assert t >= quantum and dim % t == 0, (dim, pref, quantum)
    return t


def _params(*sem):
    return pltpu.CompilerParams(dimension_semantics=sem, vmem_limit_bytes=VMEM_LIMIT_BYTES)


def _dot(a, b):
    return jnp.dot(a, b, preferred_element_type=F32)


def _dot_nt(a, b):
    return lax.dot_general(a, b, (((1,), (1,)), ((), ())), preferred_element_type=F32)


def _dot_tn(a, b):
    return lax.dot_general(a, b, (((0,), (0,)), ((), ())), preferred_element_type=F32)


def _rmsnorm_kernel(x_ref, g_ref, o_ref):
    x = x_ref[...]
    ms = jnp.mean(x * x, axis=-1, keepdims=True)
    o_ref[...] = (x * lax.rsqrt(ms + EPS) * g_ref[...]).astype(o_ref.dtype)


def _rmsnorm(x2d, g, out_dtype):
    T, D = x2d.shape
    tm = _tile(T, TILE_PREFS["norm_m"], SUBLANES)
    return pl.pallas_call(
        _rmsnorm_kernel,
        grid=(T // tm,),
        in_specs=[pl.BlockSpec((tm, D), lambda i: (i, 0)), pl.BlockSpec((1, D), lambda i: (0, 0))],
        out_specs=pl.BlockSpec((tm, D), lambda i: (i, 0)),
        out_shape=jax.ShapeDtypeStruct((T, D), out_dtype),
        compiler_params=_params("parallel"),
        name="rmsnorm",
    )(x2d, g.reshape(1, D).astype(F32))


def _row_tiled(rs, width):
    return jnp.tile(rs, (1, width // LANES))


def _mm_kernel(*refs, has_row_scale, has_col_scale):
    a_ref, w_ref, o_ref = refs[0], refs[1], refs[-1]
    u = _dot(a_ref[...], w_ref[...].astype(BF16))
    if has_row_scale:
        u = u * _row_tiled(refs[2][...], u.shape[1])
    if has_col_scale:
        u = u * refs[-2][...]
    o_ref[...] = u.astype(o_ref.dtype)


def _mm_res_kernel(a_ref, w_ref, r_ref, o_ref):
    o_ref[...] = r_ref[...] + _dot(a_ref[...], w_ref[...].astype(BF16))


def _mm_res_norm_kernel(a_ref, w_ref, r_ref, g_ref, o_ref, xg_ref, rs_ref, *, width):
    n = pl.program_id(1)

    @pl.when(n == 0)
    def _():
        rs_ref[...] = jnp.zeros_like(rs_ref)

    x = r_ref[...] + _dot(a_ref[...], w_ref[...].astype(BF16))
    o_ref[...] = x
    xg_ref[...] = (x * g_ref[...]).astype(xg_ref.dtype)
    xx = x * x
    rs_ref[...] += functools.reduce(jnp.add, [xx[:, c:c + LANES] for c in range(0, xx.shape[1], LANES)])

    @pl.when(n == pl.num_programs(1) - 1)
    def _():
        ms = jnp.sum(rs_ref[...], axis=-1, keepdims=True) * (1.0 / width)
        rs_ref[...] = jnp.broadcast_to(lax.rsqrt(ms + EPS), rs_ref.shape)


def _matmul(a, w, out_dtype, layer=None, residual=None, row_scale=None, col_scale=None, next_gain=None,
            tm_pref=None, tn_pref=None, name="matmul"):
    M, K = a.shape
    N = w.shape[-1]
    tm = _tile(M, tm_pref or TILE_PREFS["mm_m"], SUBLANES)
    tn = _tile(N, tn_pref or TILE_PREFS["mm_n"], LANES)
    if layer is None:
        w_spec = pl.BlockSpec((K, tn), lambda m, n: (0, n))
    else:
        w_spec = pl.BlockSpec((None, K, tn), lambda m, n: (layer, 0, n))
    row_spec = pl.BlockSpec((tm, LANES), lambda m, n: (m, 0))
    col_spec = pl.BlockSpec((1, tn), lambda m, n: (0, n))
    tile_spec = pl.BlockSpec((tm, tn), lambda m, n: (m, n))
    in_specs = [pl.BlockSpec((tm, K), lambda m, n: (m, 0)), w_spec]
    args = [a, w]
    out_specs, out_shape = tile_spec, jax.ShapeDtypeStruct((M, N), out_dtype)
    if residual is None:
        assert next_gain is None
        kern = functools.partial(_mm_kernel, has_row_scale=row_scale is not None, has_col_scale=col_scale is not None)
        if row_scale is not None:
            in_specs.append(row_spec)
            args.append(row_scale)
        if col_scale is not None:
            in_specs.append(col_spec)
            args.append(col_scale.reshape(1, N).astype(F32))
    else:
        assert row_scale is None and col_scale is None and out_dtype == F32
        kern = _mm_res_kernel
        in_specs.append(tile_spec)
        args.append(residual)
        if next_gain is not None:
            kern = functools.partial(_mm_res_norm_kernel, width=N)
            in_specs.append(col_spec)
            args.append(next_gain.reshape(1, N).astype(F32))
            out_specs = [tile_spec, tile_spec, row_spec]
            out_shape = [out_shape, jax.ShapeDtypeStruct((M, N), BF16), jax.ShapeDtypeStruct((M, LANES), F32)]
    return pl.pallas_call(
        kern,
        grid=(M // tm, N // tn),
        in_specs=in_specs,
        out_specs=out_specs,
        out_shape=out_shape,
        compiler_params=_params("parallel", "arbitrary"),
        name=name,
    )(*args)


def _hgrn_consts(C):
    nlev = int(math.log2(C))
    assert 1 << nlev == C
    r = np.arange(C)
    rr, jj = r[:, None], r[None, :]
    blocks = []
    for l in range(nlev):
        c = 1 << l
        ref = ((r & ~(2 * c - 1)) + c - 1)[:, None]
        odd = ((r & c) != 0)[:, None]
        blocks.append(np.where(odd, (jj > ref) & (jj <= rr), (jj > rr) & (jj <= ref)))
    blocks.append(jj <= rr)
    blocks.append(jj > rr)
    wall = np.concatenate(blocks, axis=0).astype(np.float32)
    x = rr ^ jj
    lvl = np.where(x == 0, -1, np.floor(np.log2(np.maximum(x, 1))).astype(np.int64))
    lvl = np.where(jj > rr, -2, lvl).astype(np.int32)
    return wall, lvl, nlev


def _hgrn_kernel(q_ref, zf_ref, v_ref, zg_ref, lbl_ref, gn_ref, wall_ref, lvl_ref, o_ref, st_ref,
                 *, layer, C, nlev, n_chunks):
    @pl.when(pl.program_id(2) == 0)
    def _():
        st_ref[...] = jnp.zeros_like(st_ref)

    lbl = lbl_ref[...]
    e = jnp.exp(lbl - jnp.max(lbl, axis=0, keepdims=True))
    p = e / jnp.sum(e, axis=0, keepdims=True)
    cs = p[0:1]
    for j in range(1, layer + 1):
        cs = cs + p[j:j + 1]
    lb = jnp.maximum(cs - p[0:1], 0.0)
    gn = gn_ref[...]
    lvl = lvl_ref[...]

    sts = [st_ref[0], st_ref[1]]
    for c in range(n_chunks):
        rows = slice(c * C, (c + 1) * C)
        zf = zf_ref[rows, :]
        f = lb + (1.0 - lb) * jax.nn.sigmoid(zf)
        g = jnp.log(jnp.maximum(f, F_FLOOR))
        kin2 = (1.0 - lb) * jax.nn.sigmoid(-zf)
        g_hi = g.astype(BF16)
        g_lo = (g - g_hi.astype(F32)).astype(BF16)
        d2 = _dot(wall_ref[...], jnp.concatenate([g_hi, g_lo], axis=0))
        og2 = []
        for hd in range(2):
            head = slice(hd * HEAD_DIM, (hd + 1) * HEAD_DIM)
            d = d2[:, head]
            q = q_ref[rows, head]
            v = v_ref[rows, head].astype(BF16)
            kin = kin2[:, head]
            scores = jnp.where(lvl == -1, _dot_nt(q.astype(BF16), kin.astype(BF16)), 0.0)
            for l in range(nlev):
                el = jnp.exp(d[l * C:(l + 1) * C])
                pl_ = _dot_nt((q * el).astype(BF16), (kin * el).astype(BF16))
                scores = jnp.where(lvl == l, pl_, scores)
            b = d[nlev * C:(nlev + 1) * C]
            b_rest = d[(nlev + 1) * C:(nlev + 2) * C]

            st = sts[hd]
            inter = _dot_nt((q * jnp.exp(b)).astype(BF16), st.astype(BF16))
            intra = _dot(scores.astype(BF16), v)
            sts[hd] = st * jnp.exp(b[C - 1:C, :]) + _dot_tn(v, (kin * jnp.exp(b_rest)).astype(BF16))

            og = (inter + intra) * jax.nn.sigmoid(zg_ref[rows, head])
            ms = jnp.mean(og * og, axis=-1, keepdims=True)
            og2.append(og * lax.rsqrt(ms + EPS))
        o_ref[rows, :] = (jnp.concatenate(og2, axis=1) * gn).astype(o_ref.dtype)
    st_ref[0] = sts[0]
    st_ref[1] = sts[1]


def _hgrn_mixer(proj, lb_logits, gnorm, layer, D):
    B, S, _ = proj.shape
    depth, MIX = lb_logits.shape
    W = 2 * HEAD_DIM
    HP = MIX // W
    assert MIX % W == 0
    C = min(HGRN_CHUNK, S)
    L = _tile(S, max(TILE_PREFS["hgrn_l"], C), C)
    wall, lvl, nlev = _hgrn_consts(C)
    wall = np.concatenate([wall, wall], axis=1)
    blk = lambda off: pl.BlockSpec((None, L, W), lambda b, h, c: (b, c, off + h))
    return pl.pallas_call(
        functools.partial(_hgrn_kernel, layer=layer, C=C, nlev=nlev, n_chunks=L // C),
        grid=(B, HP, S // L),
        in_specs=[blk(0), blk(HP), blk(2 * HP), blk(3 * HP),
                  pl.BlockSpec((depth, W), lambda b, h, c: (0, h)),
                  pl.BlockSpec((1, W), lambda b, h, c: (0, h)),
                  pl.BlockSpec(wall.shape, lambda b, h, c: (0, 0)),
                  pl.BlockSpec(lvl.shape, lambda b, h, c: (0, 0))],
        out_specs=pl.BlockSpec((None, L, W), lambda b, h, c: (b, c, h)),
        out_shape=jax.ShapeDtypeStruct((B, S, D), BF16),
        scratch_shapes=[pltpu.VMEM((2, HEAD_DIM, HEAD_DIM), F32)],
        compiler_params=_params("parallel", "parallel", "arbitrary"),
        name="hgrn2_mixer",
    )(proj, proj, proj, proj, lb_logits.astype(F32), gnorm.reshape(1, MIX).astype(F32),
      jnp.asarray(wall, BF16), jnp.asarray(lvl))


def _sb_consts():
    r = np.arange(LANES)
    u = np.concatenate([(r[:, None] > r[None, :]), np.ones((LANES, LANES), bool)], axis=1)
    return np.concatenate([u, u], axis=0).astype(np.float32)


def _sb_kernel(q_ref, k_ref, v_ref, u_ref, o_ref, z_ref, w_ref, *, T, G):
    i = pl.program_id(2)
    uu = u_ref[...]

    def rows_of(j):
        return slice(j * T, (j + 1) * T) if isinstance(j, int) else pl.ds(pl.multiple_of(j * T, T), T)

    def lanes_of(g):
        return slice(g * HEAD_DIM, (g + 1) * HEAD_DIM)

    def scores(g, j):
        return _dot_nt(q_ref[:, lanes_of(g)], k_ref[rows_of(j), lanes_of(g)])

    def weighted_values(g, w, j):
        return _dot(w, v_ref[rows_of(j), lanes_of(g)])

    def weights(z, carry, diagonal):
        neg_abs = lax.bitcast_convert_type(lax.bitcast_convert_type(z, jnp.uint32) | jnp.uint32(0x80000000), F32)
        sp = jnp.maximum(z, 0.0) + jnp.log(1.0 + jnp.exp2(neg_abs)) * LOG2E
        log_beta = z - sp
        if diagonal:
            strictly_earlier = (lax.broadcasted_iota(jnp.int32, (T, T), 0)
                                > lax.broadcasted_iota(jnp.int32, (T, T), 1))
            sp = jnp.where(strictly_earlier, sp, 0.0)
        hi = sp.astype(BF16)
        lo = (sp - hi.astype(F32)).astype(BF16)
        ws = [None] * (T // LANES)
        for c in reversed(range(T // LANES)):
            seg = slice(c * LANES, (c + 1) * LANES)
            cs = _dot(jnp.concatenate([hi[:, seg], lo[:, seg]], axis=1), uu)
            w = jnp.exp2(log_beta[:, seg] - cs[:, :LANES] - carry)
            if diagonal:
                w = jnp.where(strictly_earlier[:, seg], w, 0.0)
            ws[c] = w.astype(BF16)
            carry = carry + cs[:, LANES:]
        return jnp.concatenate(ws, axis=1), carry

    def first(g):
        w_ref[g], carry = weights(scores(g, i), jnp.zeros((T, LANES), F32), True)
        z_ref[0, g] = scores(g, jnp.maximum(i - 1, 0))
        return carry, jnp.zeros((T, HEAD_DIM), F32)

    def step(g, j, slot, carry, acc):
        acc = acc + weighted_values(g, w_ref[g], j + 1)
        w, carry = weights(z_ref[slot, g], carry, False)
        z_ref[1 - slot, g] = scores(g, jnp.maximum(j - 1, 0))
        w_ref[g] = w
        return carry, acc

    def two_tiles(p, state):
        j = i - 1 - 2 * p
        state = tuple(step(g, j, 0, *state[g]) for g in range(G))
        return tuple(step(g, j - 1, 1, *state[g]) for g in range(G))

    def last_tile(_, state):
        return tuple(step(g, 0, 0, *state[g]) for g in range(G))

    state = tuple(first(g) for g in range(G))
    state = lax.fori_loop(0, i // 2, two_tiles, state)
    state = lax.fori_loop(0, i % 2, last_tile, state)
    for g in range(G):
        o_ref[:, lanes_of(g)] = (state[g][1] + weighted_values(g, w_ref[g], 0)).astype(o_ref.dtype)


def _sb_mixer(proj, MIX, D):
    B, S, _ = proj.shape
    H = MIX // HEAD_DIM
    T = _tile(S, SB_TILE, LANES)
    G = SB_HEADS_PER_STEP
    assert H % G == 0
    HG = H // G
    W = G * HEAD_DIM
    u = _sb_consts()
    return pl.pallas_call(
        functools.partial(_sb_kernel, T=T, G=G),
        grid=(B, HG, S // T),
        in_specs=[pl.BlockSpec((None, T, W), lambda b, h, i: (b, i, h)),
                  pl.BlockSpec((None, S, W), lambda b, h, i: (b, 0, HG + h)),
                  pl.BlockSpec((None, S, W), lambda b, h, i: (b, 0, 2 * HG + h)),
                  pl.BlockSpec(u.shape, lambda b, h, i: (0, 0))],
        out_specs=pl.BlockSpec((None, T, W), lambda b, h, i: (b, i, h)),
        out_shape=jax.ShapeDtypeStruct((B, S, D), BF16),
        scratch_shapes=[pltpu.VMEM((2, G, T, T), F32), pltpu.VMEM((G, T, T), BF16)],
        compiler_params=_params("parallel", "parallel", "arbitrary"),
        name="stickbreaking_mixer",
    )(proj, proj, proj, jnp.asarray(u, BF16))


def _memattn_kernel(q_ref, mk_ref, mv_ref, mixed_ref, o_ref, *, scale):
    del mixed_ref
    s = _dot_nt(q_ref[...].astype(BF16), mk_ref[...]) * scale
    e = jnp.exp(s - jnp.max(s, axis=-1, keepdims=True))
    p = e / jnp.sum(e, axis=-1, keepdims=True)
    o_ref[...] = _dot(p.astype(BF16), mv_ref[...]).astype(o_ref.dtype)


def _memattn(proj, kv, mixed, MIX):
    B, S, P = proj.shape
    D = mixed.shape[-1]
    MEMW = D - MIX
    dm = MEMW // MEM_HEADS
    NM = kv.shape[1]
    ts = _tile(S, TILE_PREFS["mem_s"], SUBLANES)
    q_off = (P - MEMW) // dm
    o_off = MIX // dm
    return pl.pallas_call(
        functools.partial(_memattn_kernel, scale=dm ** -0.5),
        grid=(B, MEM_HEADS, S // ts),
        in_specs=[pl.BlockSpec((None, ts, dm), lambda b, h, t: (b, t, q_off + h)),
                  pl.BlockSpec((None, NM, dm), lambda b, h, t: (b, 0, h)),
                  pl.BlockSpec((None, NM, dm), lambda b, h, t: (b, 0, MEM_HEADS + h)),
                  pl.BlockSpec(memory_space=pl.ANY)],
        out_specs=pl.BlockSpec((None, ts, dm), lambda b, h, t: (b, t, o_off + h)),
        out_shape=jax.ShapeDtypeStruct(mixed.shape, mixed.dtype),
        input_output_aliases={3: 0},
        compiler_params=_params("parallel", "parallel", "parallel"),
        name="memory_attention",
    )(proj, kv, kv, mixed)


def _ffn_up_kernel(h_ref, rs_ref, wg_ref, wv_ref, cwg_ref, cwv_ref, cbg_ref, cbv_ref, o_ref,
                   ug0_ref, uv0_ref, ug1_ref, uv1_ref, hg_ref, hv_ref, *, tm, nt, tiles_per_seq):
    s = pl.program_id(0)

    @pl.when(s == 0)
    def _():
        ug1_ref[...] = jnp.zeros_like(ug1_ref)
        uv1_ref[...] = jnp.zeros_like(uv1_ref)
        hg_ref[...] = jnp.zeros_like(hg_ref)
        hv_ref[...] = jnp.zeros_like(hv_ref)

    e = jnp.maximum(s - 1, 0)
    n = e % nt
    seq_start = ((e // nt) % tiles_per_seq) == 0

    def body(read_g, read_v, write_g, write_v):
        for u_ref, halo_ref in ((read_g, hg_ref), (read_v, hv_ref)):
            u_ref[0:SUBLANES, :] = jnp.where(seq_start, 0.0, halo_ref[n])
            halo_ref[n] = u_ref[tm:tm + SUBLANES, :]

        def conv_rows(cw_ref, cb_ref, u_ref, r0, nr):
            cw = cw_ref[...]
            return (cb_ref[...] + cw[2:3] * u_ref[SUBLANES + r0:SUBLANES + r0 + nr, :]
                    + cw[1:2] * u_ref[SUBLANES - 1 + r0:SUBLANES - 1 + r0 + nr, :]
                    + cw[0:1] * u_ref[SUBLANES - 2 + r0:SUBLANES - 2 + r0 + nr, :])

        KT = MXU_K
        n_chunks = h_ref.shape[1] // KT
        rows = tm // n_chunks
        acc = [None, None]
        for c in range(n_chunks):
            gate = conv_rows(cwg_ref, cbg_ref, read_g, c * rows, rows)
            val = conv_rows(cwv_ref, cbv_ref, read_v, c * rows, rows)
            o_ref[c * rows:(c + 1) * rows, :] = (gate * jax.nn.sigmoid(gate) * val).astype(o_ref.dtype)
            hk = h_ref[:, c * KT:(c + 1) * KT]
            for b, w_ref in enumerate((wg_ref, wv_ref)):
                part = _dot(hk, w_ref[c * KT:(c + 1) * KT, :].astype(BF16))
                acc[b] = part if acc[b] is None else acc[b] + part
        rs = _row_tiled(rs_ref[...], o_ref.shape[1])
        write_g[SUBLANES:SUBLANES + tm, :] = acc[0] * rs
        write_v[SUBLANES:SUBLANES + tm, :] = acc[1] * rs

    @pl.when(s % 2 == 0)
    def _():
        body(ug1_ref, uv1_ref, ug0_ref, uv0_ref)

    @pl.when(s % 2 == 1)
    def _():
        body(ug0_ref, uv0_ref, ug1_ref, uv1_ref)


def _ffn_up(h, row_scale, w_up, conv_w, conv_b, layer, S):
    T, D = h.shape
    F = w_up.shape[-1] // 2
    tm = _tile(S, TILE_PREFS["up_m"], SUBLANES)
    tn = _tile(F, TILE_PREFS["up_n"], LANES)
    nt = F // tn
    assert conv_w.shape[1] == CONV_W and conv_b.shape[1] == 1
    n_tiles = (T // tm) * nt
    mm = lambda s: jnp.minimum(s, n_tiles - 1)
    ep = lambda s: jnp.maximum(s - 1, 0)
    u_scratch = pltpu.VMEM((tm + SUBLANES, tn), F32)
    halo_scratch = pltpu.VMEM((nt, SUBLANES, tn), F32)
    return pl.pallas_call(
        functools.partial(_ffn_up_kernel, tm=tm, nt=nt, tiles_per_seq=S // tm),
        grid=(n_tiles + 1,),
        in_specs=[pl.BlockSpec((tm, D), lambda s: (mm(s) // nt, 0)),
                  pl.BlockSpec((tm, LANES), lambda s: (mm(s) // nt, 0)),
                  pl.BlockSpec((None, D, tn), lambda s: (layer, 0, mm(s) % nt)),
                  pl.BlockSpec((None, D, tn), lambda s: (layer, 0, nt + mm(s) % nt)),
                  pl.BlockSpec((None, CONV_W, tn), lambda s: (layer, 0, ep(s) % nt)),
                  pl.BlockSpec((None, CONV_W, tn), lambda s: (layer, 0, nt + ep(s) % nt)),
                  pl.BlockSpec((None, 1, tn), lambda s: (layer, 0, ep(s) % nt)),
                  pl.BlockSpec((None, 1, tn), lambda s: (layer, 0, nt + ep(s) % nt))],
        out_specs=pl.BlockSpec((tm, tn), lambda s: (ep(s) // nt, ep(s) % nt)),
        out_shape=jax.ShapeDtypeStruct((T, F), BF16),
        scratch_shapes=[u_scratch, u_scratch, u_scratch, u_scratch, halo_scratch, halo_scratch],
        compiler_params=_params("arbitrary"),
        name="ffn_up_conv_gate",
    )(h, row_scale, w_up, w_up, conv_w, conv_w, conv_b, conv_b)


def kernel(x, mem, mem_norm, mem_w_kv, lb_logits, mix_norm, hgrn_w_in, hgrn_norm, hgrn_w_out, sb_w_in, sb_w_out,
           ffn_norm, ffn_w_up, ffn_conv_w, ffn_conv_b, ffn_w_down, final_norm):
    B, S, D = x.shape
    T = B * S
    depth, MIX = lb_logits.shape
    NM = mem.shape[1]

    hm = _rmsnorm(mem.reshape(B * NM, D), mem_norm, BF16)
    kv = _matmul(hm, mem_w_kv, BF16, name="mem_kv_proj").reshape(B, NM, -1)

    sb_cols = sb_w_in.shape[-1]
    sb_scale = jnp.where(jnp.arange(sb_cols) < MIX, LOG2E / math.sqrt(HEAD_DIM), 1.0).astype(F32)
    w_down = ffn_w_down.astype(BF16)
    conv_b = ffn_conv_b.reshape(depth, 1, -1).astype(F32)
    conv_w = ffn_conv_w.astype(F32)

    x2 = x.reshape(T, D)
    h = _rmsnorm(x2, mix_norm[0], BF16)
    row_scale = None
    for i in range(depth):
        j = i // 2
        if i % 2 == 0:
            proj = _matmul(h, hgrn_w_in, F32, layer=j, row_scale=row_scale, name="hgrn_in_proj").reshape(B, S, -1)
            mixed = _hgrn_mixer(proj, lb_logits, hgrn_norm[j], i, D)
            w_out = hgrn_w_out
        else:
            proj = _matmul(h, sb_w_in, BF16, layer=j, row_scale=row_scale, col_scale=sb_scale,
                           name="sb_in_proj").reshape(B, S, -1)
            mixed = _sb_mixer(proj, MIX, D)
            w_out = sb_w_out
        mixed = _memattn(proj, kv, mixed, MIX)
        x2, h, row_scale = _matmul(mixed.reshape(T, D), w_out, F32, layer=j, residual=x2, next_gain=ffn_norm[i],
                                   name="out_proj")

        act = _ffn_up(h, row_scale, ffn_w_up, conv_w, conv_b, i, S)
        down = dict(layer=i, residual=x2, tm_pref=TILE_PREFS["down_m"], tn_pref=TILE_PREFS["down_n"],
                    name="ffn_down_proj")
        if i + 1 < depth:
            x2, h, row_scale = _matmul(act, w_down, F32, next_gain=mix_norm[i + 1], **down)
        else:
            x2 = _matmul(act, w_down, F32, **down)

    return _rmsnorm(x2, final_norm, F32).reshape(B, S, D)
```
